```python
import jax
import jax.numpy as jnp
from jax import lax
import numpy as np

D_MODEL = 1024
BATCH = 8
SEQ = 4096
DEPTH = 4

HEAD_DIM = 64
N_MIXERS = 4
GROUP_W = D_MODEL // N_MIXERS
GROUP_HEADS = GROUP_W // HEAD_DIM
CONV_K = 31
GLA_DK = HEAD_DIM // 2
GLA_RANK = 16
GLA_TAU = 16.0
GLA_CHUNK = 64
DIL_PAIRS = ((128, 1), (512, 4), (2048, 16))
DIL_BLOCK = 128
ROPE_THETA = 10000.0
SGU_CHUNK = 128
D_FF = 2816
FFN_CONV_K = 3
EPS = 1e-6
NEG_INF = -1e30

IN_WIDTHS = (
    GROUP_W, GROUP_W,
    GROUP_HEADS * GLA_DK, GROUP_HEADS * GLA_DK,
    GROUP_W, GROUP_W, GLA_RANK,
    GROUP_W, GROUP_W, GROUP_W,
    GROUP_W, GROUP_W,
)
N_IN = sum(IN_WIDTHS)

kernel_name = 'hybrid_four_mixer_block'


def rmsnorm(x, g):
    x32 = x.astype(jnp.float32)
    y = x32 * lax.rsqrt(jnp.mean(x32 * x32, axis=-1, keepdims=True) + EPS)
    return (y * g.astype(jnp.float32)).astype(x.dtype)


def layernorm(x, g, b):
    x32 = x.astype(jnp.float32)
    mu = jnp.mean(x32, axis=-1, keepdims=True)
    var = jnp.mean(jnp.square(x32 - mu), axis=-1, keepdims=True)
    y = (x32 - mu) * lax.rsqrt(var + EPS)
    return (y * g.astype(jnp.float32) + b.astype(jnp.float32)).astype(x.dtype)


def causal_dwconv(x, w, b):
    K, C = w.shape
    xp = jnp.pad(x, ((0, 0), (K - 1, 0), (0, 0)))
    y = lax.conv_general_dilated(xp, w[:, None, :].astype(x.dtype), window_strides=(1,),
                                 padding='VALID', dimension_numbers=('NWC', 'WIO', 'NWC'),
                                 feature_group_count=C)
    return y + b.astype(x.dtype)


def rotary(t, cos, sin):
    t1, t2 = jnp.split(t, 2, axis=-1)
    return jnp.concatenate([t1 * cos - t2 * sin, t2 * cos + t1 * sin], axis=-1)


def conformer_conv(a_val, a_gate, conv_w, conv_b, ln_g, ln_b):
    h = a_val * jax.nn.sigmoid(a_gate)
    h = causal_dwconv(h, conv_w, conv_b)
    h = layernorm(h, ln_g, ln_b)
    return jax.nn.silu(h)


def gla_chunked(q, k, v, gk):
    B, S, H, dk = q.shape
    dv = v.shape[-1]
    nc = S // GLA_CHUNK

    def chunks(t):
        return t.reshape(B, nc, GLA_CHUNK, H, t.shape[-1]).transpose(1, 0, 3, 2, 4)

    q, k, v, gk = chunks(q), chunks(k), chunks(v), chunks(gk)
    b = jnp.cumsum(gk, axis=3)
    b_last = b[:, :, :, -1:, :]
    q_t = q * jnp.exp(b)
    k_t = k * jnp.exp(-b)
    k_end = k * jnp.exp(b_last - b)
    causal = jnp.tril(jnp.ones((GLA_CHUNK, GLA_CHUNK), dtype=bool))
    att = jnp.where(causal, jnp.einsum('nbhie,nbhje->nbhij', q_t, k_t), 0.0)
    o_intra = jnp.einsum('nbhij,nbhjv->nbhiv', att, v)

    def step(state, inp):
        k_e, v_c, dec = inp
        new = dec[:, :, 0, :, None] * state + jnp.einsum('bhje,bhjv->bhev', k_e, v_c)
        return new, state

    s0 = jnp.zeros((B, H, dk, dv), jnp.float32)
    _, s_prev = lax.scan(step, s0, (k_end, v, jnp.exp(b_last)))
    o_inter = jnp.einsum('nbhie,nbhev->nbhiv', q_t, s_prev)
    return (o_intra + o_inter).transpose(1, 0, 3, 2, 4).reshape(B, S, H, dv)


def gla_mixer(q, k, v, g, lr, w2, gb, norm_g):
    B, S, _ = q.shape
    f32 = jnp.float32
    q = q.reshape(B, S, GROUP_HEADS, GLA_DK).astype(f32) * GLA_DK ** -0.5
    k = k.reshape(B, S, GROUP_HEADS, GLA_DK).astype(f32)
    v = v.reshape(B, S, GROUP_HEADS, HEAD_DIM).astype(f32)
    gk = jax.nn.log_sigmoid((lr @ w2 + gb).astype(f32)) / GLA_TAU
    o = gla_chunked(q, k, v, gk.reshape(B, S, GROUP_HEADS, GLA_DK))
    o = o * lax.rsqrt(jnp.mean(o * o, axis=-1, keepdims=True) + EPS) * norm_g.astype(f32)
    o = o.reshape(B, S, GROUP_W) * jax.nn.silu(g.astype(f32))
    return o.astype(g.dtype)


def dilated_branch(q, k, v, dil, back):
    B, S, H, E = q.shape
    L = S // dil
    nb = -(-L // DIL_BLOCK)
    Lp = nb * DIL_BLOCK

    def strided(t):
        t = t.reshape(B, L, dil, H, E).transpose(0, 2, 3, 1, 4)
        t = jnp.pad(t, ((0, 0), (0, 0), (0, 0), (0, Lp - L), (0, 0)))
        return t.reshape(B, dil, H, nb, DIL_BLOCK, E)

    def with_prev(t):
        prev = jnp.pad(t[:, :, :, :-1], ((0, 0), (0, 0), (0, 0), (1, 0), (0, 0), (0, 0)))
        return jnp.concatenate([prev, t], axis=4)

    qs = strided(q)
    kk = with_prev(strided(k))
    vv = with_prev(strided(v))
    s = jnp.einsum('bdhnqe,bdhnke->bdhnqk', qs, kk)
    i = jnp.arange(DIL_BLOCK)[:, None]
    j = jnp.arange(2 * DIL_BLOCK)[None, :]
    rel = i - j + DIL_BLOCK
    band = (rel >= 0) & (rel <= back)
    valid = (jnp.arange(nb)[:, None, None] > 0) | (j[None] >= DIL_BLOCK)
    mask = band[None] & valid
    s = jnp.where(mask, s, NEG_INF)
    m = jnp.max(s, axis=-1, keepdims=True)
    p = jnp.exp(s - m)
    den = jnp.sum(p, axis=-1, keepdims=True)
    o = jnp.einsum('bdhnqk,bdhnke->bdhnqe', p, vv) / den
    lse = m + jnp.log(den)

    def unstride(t):
        t = t.reshape(B, dil, H, Lp, t.shape[-1])[:, :, :, :L]
        return t.transpose(0, 3, 1, 2, 4).reshape(B, S, H, t.shape[-1])

    return unstride(o), unstride(lse)[..., 0]


def dilated_attention(q, k, v, cos, sin):
    B, S, _ = q.shape
    shp = (B, S, GROUP_HEADS, HEAD_DIM)
    f32 = jnp.float32
    q = rotary(q.reshape(shp).astype(f32), cos, sin) * HEAD_DIM ** -0.5
    k = rotary(k.reshape(shp).astype(f32), cos, sin)
    v = v.reshape(shp).astype(f32)
    outs, lses = [], []
    for window, dil in DIL_PAIRS:
        o, lse = dilated_branch(q, k, v, dil, window // dil)
        outs.append(o)
        lses.append(lse)
    w = jax.nn.softmax(jnp.stack(lses), axis=0)
    o = jnp.einsum('nbsh,nbshe->bshe', w, jnp.stack(outs))
    return o.reshape(B, S, GROUP_W)


def sgu_mixer(u, v, ln_g, ln_b, w_s, b_s):
    u = jax.nn.gelu(u)
    v = layernorm(jax.nn.gelu(v), ln_g, ln_b)
    B, S, _ = v.shape
    n = S // SGU_CHUNK
    vc = v.reshape(B, n, SGU_CHUNK, GROUP_HEADS, HEAD_DIM)
    causal = jnp.tril(jnp.ones((SGU_CHUNK, SGU_CHUNK), dtype=bool))
    w = jnp.where(causal[None], w_s, 0.0).astype(v.dtype)
    sv = jnp.einsum('gts,bnsge->bntge', w, vc) + b_s.T[:, :, None].astype(v.dtype)
    return u * sv.reshape(B, S, GROUP_W)


def conv_ffn(h, w_up, conv_w, conv_b, w_down):
    z = causal_dwconv(h @ w_up, conv_w, conv_b)
    a, b = jnp.split(z, 2, axis=-1)
    return (jax.nn.silu(a) * b) @ w_down


def setup_inputs(seed: int = 0) -> dict:
    key = jax.random.key(seed)
    ks = jax.random.split(key, 24)
    L = DEPTH

    def nrm(k, shape, scale):
        return jax.random.normal(k, shape, jnp.float32) * scale

    def gain(k, shape):
        return 1.0 + 0.02 * jax.random.normal(k, shape, jnp.float32)

    return {
        'x': nrm(ks[0], (BATCH, SEQ, D_MODEL), 1.0),
        'c': nrm(ks[1], (BATCH, D_MODEL), 1.0),
        'ada_w': nrm(ks[2], (L, D_MODEL, 6 * D_MODEL), 0.5 * D_MODEL ** -0.5),
        'ada_b': nrm(ks[3], (L, 6 * D_MODEL), 0.02),
        'ln1_g': gain(ks[4], (L, D_MODEL)),
        'w_in': nrm(ks[5], (L, D_MODEL, N_IN), D_MODEL ** -0.5),
        'conv_w': nrm(ks[6], (L, CONV_K, GROUP_W), CONV_K ** -0.5),
        'conv_b': nrm(ks[7], (L, GROUP_W), 0.02),
        'cln_g': gain(ks[8], (L, GROUP_W)),
        'cln_b': nrm(ks[9], (L, GROUP_W), 0.02),
        'gla_w2': nrm(ks[10], (L, GLA_RANK, GROUP_HEADS * GLA_DK), GLA_RANK ** -0.5),
        'gla_b': nrm(ks[11], (L, GROUP_HEADS * GLA_DK), 0.02),
        'gla_norm_g': gain(ks[12], (L, HEAD_DIM)),
        'sgu_ln_g': gain(ks[13], (L, GROUP_W)),
        'sgu_ln_b': nrm(ks[14], (L, GROUP_W), 0.02),
        'sgu_w': nrm(ks[15], (L, GROUP_HEADS, SGU_CHUNK, SGU_CHUNK), 0.5 * SGU_CHUNK ** -0.5),
        'sgu_b': gain(ks[16], (L, GROUP_HEADS, SGU_CHUNK)),
        'w_out': nrm(ks[17], (L, D_MODEL, D_MODEL), D_MODEL ** -0.5),
        'ln2_g': gain(ks[18], (L, D_MODEL)),
        'ffn_up': nrm(ks[19], (L, D_MODEL, 2 * D_FF), D_MODEL ** -0.5),
        'ffn_conv_w': nrm(ks[20], (L, FFN_CONV_K, 2 * D_FF), FFN_CONV_K ** -0.5),
        'ffn_conv_b': nrm(ks[21], (L, 2 * D_FF), 0.02),
        'ffn_down': nrm(ks[22], (L, D_FF, D_MODEL), D_FF ** -0.5),
        'lnf_g': gain(ks[23], (D_MODEL,)),
    }


def reference(x, c, ada_w, ada_b, ln1_g, w_in, conv_w, conv_b, cln_g, cln_b,
              gla_w2, gla_b, gla_norm_g, sgu_ln_g, sgu_ln_b, sgu_w, sgu_b, w_out,
              ln2_g, ffn_up, ffn_conv_w, ffn_conv_b, ffn_down, lnf_g):
    B, S, _ = x.shape
    dt = x.dtype
    inv = 1.0 / (ROPE_THETA ** (jnp.arange(0, HEAD_DIM, 2, dtype=jnp.float32) / HEAD_DIM))
    ang = jnp.arange(S, dtype=jnp.float32)[:, None] * inv[None, :]
    cos = jnp.cos(ang)[None, :, None, :]
    sin = jnp.sin(ang)[None, :, None, :]
    cond = jax.nn.silu(c)
    splits = np.cumsum(IN_WIDTHS)[:-1].tolist()

    for l in range(DEPTH):
        mod = cond @ ada_w[l] + ada_b[l]
        sh1, sc1, g1, sh2, sc2, g2 = [m[:, None, :] for m in jnp.split(mod, 6, axis=-1)]

        h = rmsnorm(x, ln1_g[l]) * (1 + sc1) + sh1
        (a_val, a_gate, b_q, b_k, b_v, b_g, b_lr,
         c_q, c_k, c_v, d_u, d_v) = jnp.split(h @ w_in[l], splits, axis=-1)
        o_a = conformer_conv(a_val, a_gate, conv_w[l], conv_b[l], cln_g[l], cln_b[l])
        o_b = gla_mixer(b_q, b_k, b_v, b_g, b_lr, gla_w2[l], gla_b[l], gla_norm_g[l])
        o_c = dilated_attention(c_q, c_k, c_v, cos, sin)
        o_d = sgu_mixer(d_u, d_v, sgu_ln_g[l], sgu_ln_b[l], sgu_w[l], sgu_b[l])
        mix = jnp.concatenate([o_a, o_b, o_c, o_d], axis=-1).astype(dt) @ w_out[l]
        x = x + g1 * mix

        h = rmsnorm(x, ln2_g[l]) * (1 + sc2) + sh2
        x = x + g2 * conv_ffn(h, ffn_up[l], ffn_conv_w[l], ffn_conv_b[l], ffn_down[l])

    return rmsnorm(x, lnf_g)
```

```python
import functools

import numpy as np
import jax
import jax.numpy as jnp
from jax import lax
from jax.experimental import pallas as pl
from jax.experimental.pallas import tpu as pltpu

D_MODEL = 1024
GROUP_W = 256
GROUP_HEADS = 4
HEAD_DIM = 64
CONV_K = 31
GLA_DK = 32
GLA_RANK = 16
GLA_TAU = 16.0
GLA_CHUNK = 64
DIL_PAIRS = ((128, 1), (512, 4), (2048, 16))
DIL_BLOCK = 128
ROPE_THETA = 10000.0
SGU_CHUNK = 128
D_FF = 2816
FFN_CONV_K = 3
EPS = 1e-6
NEG_INF = -1e30

LANES = 128
BF16 = jnp.bfloat16
F32 = jnp.float32

SEC_A = (0, 512)
SEC_B = (512, 1408)
SEC_C = (1408, 2176)
SEC_D = (2176, 2688)
N_PROJ = 2688
PB_W = SEC_B[1] - SEC_B[0]
PC_W = 384
N_DIL_TILES = 16

VMEM_LIMIT = 56 * 1024 * 1024


def _cparams(sem, vmem=None):
    return pltpu.CompilerParams(dimension_semantics=sem, vmem_limit_bytes=vmem)


def _dot(a, b):
    return jnp.dot(a, b, preferred_element_type=F32)


def _dot_nt(a, b):
    return lax.dot_general(a, b, (((1,), (1,)), ((), ())), preferred_element_type=F32)


def _dot_tn(a, b):
    return lax.dot_general(a, b, (((0,), (0,)), ((), ())), preferred_element_type=F32)


def _sigmoid(x):
    return 1.0 / (1.0 + jnp.exp(-x))


def _silu(x):
    return x * _sigmoid(x)


def _gelu_tanh(x):
    c = np.sqrt(2.0 / np.pi).astype(np.float32)
    return x * (0.5 * (1.0 + jnp.tanh(c * (x + 0.044715 * (x * x * x)))))


def _rms_mod(x, g, scale, shift):
    ms = jnp.mean(x * x, axis=-1, keepdims=True)
    y = x * lax.rsqrt(ms + EPS) * g
    return y * (1.0 + scale) + shift


def _layernorm(x, g, b):
    mu = jnp.mean(x, axis=-1, keepdims=True)
    xc = x - mu
    var = jnp.mean(xc * xc, axis=-1, keepdims=True)
    return xc * lax.rsqrt(var + EPS) * g + b


def _mod_kernel(c_ref, w_ref, b_ref, o_ref):
    cond = _silu(c_ref[...]).astype(BF16)
    o_ref[...] = _dot(cond, w_ref[...].astype(BF16)) + b_ref[...]


def _modulation(c, ada_w, ada_b):
    L, D, D6 = ada_w.shape
    B = c.shape[0]
    n = D6 // D
    return pl.pallas_call(
        _mod_kernel,
        grid=(L, n),
        in_specs=[
            pl.BlockSpec((B, D), lambda l, j: (0, 0)),
            pl.BlockSpec((None, D, D), lambda l, j: (l, 0, j)),
            pl.BlockSpec((None, 1, D), lambda l, j: (l, 0, j)),
        ],
        out_specs=pl.BlockSpec((None, B, D), lambda l, j: (l, 0, j)),
        out_shape=jax.ShapeDtypeStruct((L, B, D6), F32),
        compiler_params=_cparams(("parallel", "parallel")),
        name="adaln_mod",
    )(c, ada_w, ada_b.reshape(L, 1, D6))


def _inproj_kernel(x_ref, sh_ref, sc_ref, g_ref, w_ref, cos_ref, sin_ref,
                   sw_ref, sbias_ref, slg_ref, slb_ref,
                   ha_ref, pb_ref, pc_ref, od_ref):
    tm = x_ref.shape[0]
    h = _rms_mod(x_ref[...], g_ref[...], sc_ref[...], sh_ref[...]).astype(BF16)

    za = _dot(h, w_ref[:, SEC_A[0]:SEC_A[1]])
    ha_ref[...] = (za[:, :GROUP_W] * _sigmoid(za[:, GROUP_W:])).astype(BF16)

    pb_ref[...] = _dot(h, w_ref[:, SEC_B[0]:SEC_B[1]]).astype(BF16)

    zc = _dot(h, w_ref[:, SEC_C[0]:SEC_C[1]])
    cos_t = cos_ref[...]
    sin_t = sin_ref[...]
    lane = lax.broadcasted_iota(jnp.int32, (tm, LANES), 1)
    first_half = (lane % HEAD_DIM) < (HEAD_DIM // 2)

    def rope(t):
        partner = jnp.where(first_half, pltpu.roll(t, LANES - HEAD_DIM // 2, axis=1),
                            pltpu.roll(t, HEAD_DIM // 2, axis=1))
        return t * cos_t + partner * sin_t

    for p in range(2):
        base = p * PC_W
        q = rope(zc[:, base:base + LANES]) * (HEAD_DIM ** -0.5)
        k = rope(zc[:, base + LANES:base + 2 * LANES])
        v = zc[:, base + 2 * LANES:base + 3 * LANES]
        pc_ref[p, :, 0:LANES] = q.astype(BF16)
        pc_ref[p, :, LANES:2 * LANES] = k.astype(BF16)
        pc_ref[p, :, 2 * LANES:3 * LANES] = v.astype(BF16)

    zd = _dot(h, w_ref[:, SEC_D[0]:SEC_D[1]])
    u = _gelu_tanh(zd[:, :GROUP_W])
    vn = _layernorm(_gelu_tanh(zd[:, GROUP_W:]), slg_ref[...], slb_ref[...]).astype(BF16)
    row = lax.broadcasted_iota(jnp.int32, (SGU_CHUNK, SGU_CHUNK), 0)
    col = lax.broadcasted_iota(jnp.int32, (SGU_CHUNK, SGU_CHUNK), 1)
    causal = row >= col
    wm = [jnp.where(causal, sw_ref[g], 0.0).astype(BF16) for g in range(GROUP_HEADS)]
    grp = lax.broadcasted_iota(jnp.int32, (SGU_CHUNK, GROUP_W), 1) // HEAD_DIM
    sbias = sbias_ref[...]
    for c in range(tm // SGU_CHUNK):
        rows = slice(c * SGU_CHUNK, (c + 1) * SGU_CHUNK)
        vch = vn[rows]
        sv = sbias
        for g in range(GROUP_HEADS):
            sv = sv + jnp.where(grp == g, _dot(wm[g], vch), 0.0)
        od_ref[rows, :] = (u[rows] * sv).astype(BF16)


def _inproj(x2, mod3, ln_g, wp, cos_t, sin_t, sgu_w, sgu_bias, sgu_ln_g, sgu_ln_b, S, tm):
    T, D = x2.shape
    nps = S // tm
    full = lambda *shape: pl.BlockSpec(shape, lambda i: (0,) * len(shape))
    return pl.pallas_call(
        _inproj_kernel,
        grid=(T // tm,),
        in_specs=[
            pl.BlockSpec((tm, D), lambda i: (i, 0)),
            pl.BlockSpec((None, 1, D), lambda i: (i // nps, 0, 0)),
            pl.BlockSpec((None, 1, D), lambda i: (i // nps, 0, 1)),
            full(1, D),
            full(D, N_PROJ),
            pl.BlockSpec((tm, LANES), lambda i: (i % nps, 0)),
            pl.BlockSpec((tm, LANES), lambda i: (i % nps, 0)),
            full(GROUP_HEADS, SGU_CHUNK, SGU_CHUNK),
            full(SGU_CHUNK, GROUP_W),
            full(1, GROUP_W),
            full(1, GROUP_W),
        ],
        out_specs=[
            pl.BlockSpec((tm, GROUP_W), lambda i: (i, 0)),
            pl.BlockSpec((tm, PB_W), lambda i: (i, 0)),
            pl.BlockSpec((2, tm, PC_W), lambda i: (0, i, 0)),
            pl.BlockSpec((tm, GROUP_W), lambda i: (i, 0)),
        ],
        out_shape=[
            jax.ShapeDtypeStruct((T, GROUP_W), BF16),
            jax.ShapeDtypeStruct((T, PB_W), BF16),
            jax.ShapeDtypeStruct((2, T, PC_W), BF16),
            jax.ShapeDtypeStruct((T, GROUP_W), BF16),
        ],
        compiler_params=_cparams(("parallel",), VMEM_LIMIT),
        name="inproj",
    )(x2, mod3, mod3, ln_g, wp, cos_t, sin_t, sgu_w, sgu_bias, sgu_ln_g, sgu_ln_b)


CONV_HALO = 32
CONV_SUB = 64


def _conv_kernel(cur_ref, halo_ref, w_ref, b_ref, g_ref, beta_ref, o_ref, ext_ref, *, nps):
    tc = cur_ref.shape[0]
    first = (pl.program_id(0) % nps) == 0
    halo = halo_ref[...].astype(F32)
    ext_ref[0:CONV_HALO, :] = jnp.where(first, 0.0, halo)
    ext_ref[CONV_HALO:, :] = cur_ref[...].astype(F32)
    w = w_ref[...]
    off0 = CONV_HALO - (CONV_K - 1)
    for sub in range(tc // CONV_SUB):
        r0 = sub * CONV_SUB
        acc = jnp.zeros((CONV_SUB, GROUP_W), F32) + b_ref[...]
        for k in range(CONV_K):
            acc = acc + w[k:k + 1, :] * ext_ref[r0 + off0 + k:r0 + off0 + k + CONV_SUB, :]
        y = _layernorm(acc, g_ref[...], beta_ref[...])
        o_ref[r0:r0 + CONV_SUB, :] = _silu(y).astype(BF16)


def _conv_mixer(ha, conv_w, conv_b, ln_g, ln_b, S, tc):
    T = ha.shape[0]
    nps = S // tc
    hb = tc // CONV_HALO
    full = lambda *shape: pl.BlockSpec(shape, lambda i: (0,) * len(shape))
    return pl.pallas_call(
        functools.partial(_conv_kernel, nps=nps),
        grid=(T // tc,),
        in_specs=[
            pl.BlockSpec((tc, GROUP_W), lambda i: (i, 0)),
            pl.BlockSpec((CONV_HALO, GROUP_W), lambda i: (jnp.maximum(i * hb - 1, 0), 0)),
            full(CONV_K, GROUP_W),
            full(1, GROUP_W),
            full(1, GROUP_W),
            full(1, GROUP_W),
        ],
        out_specs=pl.BlockSpec((tc, GROUP_W), lambda i: (i, 0)),
        out_shape=jax.ShapeDtypeStruct((T, GROUP_W), BF16),
        scratch_shapes=[pltpu.VMEM((tc + CONV_HALO, GROUP_W), F32)],
        compiler_params=_cparams(("parallel",)),
        name="conv_mixer",
    )(ha, ha, conv_w, conv_b, ln_g, ln_b)


def _gla_kernel(pb_ref, w2_ref, gb_ref, ng_ref, o_ref, st_ref):
    tb = pb_ref.shape[0]
    C = GLA_CHUNK
    HK = GROUP_HEADS * GLA_DK

    @pl.when(pl.program_id(1) == 0)
    def _():
        st_ref[...] = jnp.zeros_like(st_ref)

    r_i = lax.broadcasted_iota(jnp.int32, (C, C), 0)
    c_i = lax.broadcasted_iota(jnp.int32, (C, C), 1)
    ltri = jnp.where(r_i >= c_i, 1.0, 0.0).astype(BF16)
    att_r = lax.broadcasted_iota(jnp.int32, (C, GROUP_HEADS * C), 0)
    att_c = lax.broadcasted_iota(jnp.int32, (C, GROUP_HEADS * C), 1) % C
    att_causal = att_r >= att_c
    ks_head_r = lax.broadcasted_iota(jnp.int32, (GROUP_HEADS * C, HK), 0) // C
    ks_head_c = lax.broadcasted_iota(jnp.int32, (GROUP_HEADS * C, HK), 1) // GLA_DK
    ks_mask = ks_head_r == ks_head_c
    vs_head_r = lax.broadcasted_iota(jnp.int32, (GROUP_HEADS * C, GROUP_W), 0) // C
    vs_head_c = lax.broadcasted_iota(jnp.int32, (GROUP_HEADS * C, GROUP_W), 1) // HEAD_DIM
    vs_mask = vs_head_r == vs_head_c
    st_head_r = lax.broadcasted_iota(jnp.int32, (GROUP_W, HK), 0) // HEAD_DIM
    st_head_c = lax.broadcasted_iota(jnp.int32, (GROUP_W, HK), 1) // GLA_DK
    st_mask = st_head_r == st_head_c
    o_head = lax.broadcasted_iota(jnp.int32, (C, GROUP_W), 1) // HEAD_DIM

    w2 = w2_ref[...]
    gb = gb_ref[...]
    ng = ng_ref[...]

    for ch in range(tb // C):
        rows = slice(ch * C, (ch + 1) * C)
        q = pb_ref[rows, 0:HK].astype(F32)
        k = pb_ref[rows, HK:2 * HK].astype(F32)
        v = pb_ref[rows, 2 * HK:2 * HK + GROUP_W]
        g = pb_ref[rows, 2 * HK + GROUP_W:2 * HK + 2 * GROUP_W].astype(F32)
        lr = pb_ref[rows, 2 * HK + 2 * GROUP_W:2 * HK + 2 * GROUP_W + LANES]

        pre = _dot(lr, w2) + gb
        gk = (jnp.minimum(pre, 0.0) - jnp.log1p(jnp.exp(-jnp.abs(pre)))) * (1.0 / GLA_TAU)
        g1 = gk.astype(BF16)
        rem = gk - g1.astype(F32)
        g2 = rem.astype(BF16)
        g3 = (rem - g2.astype(F32)).astype(BF16)
        b = _dot(ltri, g1) + _dot(ltri, g2) + _dot(ltri, g3)
        b_last = b[C - 1:C, :]
        q_t = (q * (GLA_DK ** -0.5) * jnp.exp(b)).astype(BF16)
        k_t = (k * jnp.exp(-b)).astype(BF16)
        k_end = (k * jnp.exp(b_last - b)).astype(BF16)

        zero_bf = jnp.zeros((), BF16)
        k_stack = jnp.where(ks_mask, jnp.concatenate([k_t] * GROUP_HEADS, axis=0), zero_bf)
        att = jnp.where(att_causal, _dot_nt(q_t, k_stack), 0.0).astype(BF16)
        v_stack = jnp.where(vs_mask, jnp.concatenate([v] * GROUP_HEADS, axis=0), zero_bf)
        o_intra = _dot(att, v_stack)

        st = st_ref[...]
        o_inter = _dot_nt(q_t, st.astype(BF16))
        upd = _dot_tn(v, k_end)
        st_ref[...] = st * jnp.exp(b_last) + jnp.where(st_mask, upd, 0.0)

        o = o_intra + o_inter
        o2 = o * o
        inv = jnp.zeros((C, GROUP_W), F32)
        for hh in range(GROUP_HEADS):
            msq = jnp.sum(jnp.where(o_head == hh, o2, 0.0), axis=-1, keepdims=True) * (1.0 / HEAD_DIM)
            inv = jnp.where(o_head == hh, lax.rsqrt(msq + EPS), inv)
        o_ref[rows, :] = (o * inv * ng * _silu(g)).astype(BF16)


def _gla_mixer(pb, w2p, gb, ng, B, S, tb):
    T = pb.shape[0]
    nps = S // tb
    full = lambda *shape: pl.BlockSpec(shape, lambda b, i: (0,) * len(shape))
    return pl.pallas_call(
        _gla_kernel,
        grid=(B, nps),
        in_specs=[
            pl.BlockSpec((tb, PB_W), lambda b, i: (b * nps + i, 0)),
            full(LANES, LANES),
            full(1, LANES),
            full(1, GROUP_W),
        ],
        out_specs=pl.BlockSpec((tb, GROUP_W), lambda b, i: (b * nps + i, 0)),
        out_shape=jax.ShapeDtypeStruct((T, GROUP_W), BF16),
        scratch_shapes=[pltpu.VMEM((GROUP_W, GROUP_HEADS * GLA_DK), F32)],
        compiler_params=_cparams(("arbitrary", "arbitrary")),
        name="gla_mixer",
    )(pb, w2p, gb, ng)


def _dil_kernel(c1_ref, h1_ref, c4_ref, h4_ref, c16_ref,
                o1_ref, l1_ref, o4_ref, l4_ref, o16_ref, l16_ref):
    R = c1_ref.shape[0]
    j = pl.program_id(2)
    BLK = DIL_BLOCK
    qi = lax.broadcasted_iota(jnp.int32, (BLK, 2 * BLK), 0)
    kj = lax.broadcasted_iota(jnp.int32, (BLK, 2 * BLK), 1)
    band = (kj >= qi) & (kj <= qi + BLK)
    lane = lax.broadcasted_iota(jnp.int32, (BLK, LANES), 1)
    head0 = lane < HEAD_DIM
    zero_bf = jnp.zeros((), BF16)

    def branch(cur_ref, halo_ref, first, o_ref, l_ref):
        for blk in range(R // BLK):
            rows = slice(blk * BLK, (blk + 1) * BLK)
            q = cur_ref[rows, 0:LANES]
            if blk == 0:
                if halo_ref is None:
                    kprev = cur_ref[rows, LANES:2 * LANES]
                    vprev = cur_ref[rows, 2 * LANES:3 * LANES]
                    lim = BLK
                else:
                    kprev = halo_ref[:, LANES:2 * LANES]
                    vprev = halo_ref[:, 2 * LANES:3 * LANES]
                    lim = jnp.where(first, BLK, 0)
            else:
                prow = slice((blk - 1) * BLK, blk * BLK)
                kprev = cur_ref[prow, LANES:2 * LANES]
                vprev = cur_ref[prow, 2 * LANES:3 * LANES]
                lim = 0
            kk = jnp.concatenate([kprev, cur_ref[rows, LANES:2 * LANES]], axis=0)
            vv = jnp.concatenate([vprev, cur_ref[rows, 2 * LANES:3 * LANES]], axis=0)
            mask = band & (kj >= lim)
            outs, lses = [], []
            for hmask in (head0, jnp.logical_not(head0)):
                qm = jnp.where(hmask, q, zero_bf)
                s = jnp.where(mask, _dot_nt(qm, kk), NEG_INF)
                m = jnp.max(s, axis=-1, keepdims=True)
                p = jnp.exp(s - m)
                den = jnp.sum(p, axis=-1, keepdims=True)
                outs.append(_dot(p.astype(BF16), vv) / den)
                lses.append(m + jnp.log(den))
            o_ref[rows, :] = jnp.where(head0, outs[0], outs[1]).astype(BF16)
            l_ref[rows, :] = jnp.where(head0, lses[0], lses[1])

    branch(c1_ref, h1_ref, j == 0, o1_ref, l1_ref)
    branch(c4_ref, h4_ref, (j % 4) == 0, o4_ref, l4_ref)
    branch(c16_ref, None, True, o16_ref, l16_ref)


def _dilated_mixer(pc, B, S):
    T = B * S
    R = S // N_DIL_TILES
    hb = R // DIL_BLOCK
    v1 = pc.reshape(2, B, S, PC_W)
    v4 = pc.reshape(2, B, S // 4, 4 * PC_W)
    v16 = pc.reshape(2, B, S // 16, 16 * PC_W)
    cur = lambda f: pl.BlockSpec((None, None, R, PC_W), f)
    halo = lambda f: pl.BlockSpec((None, None, DIL_BLOCK, PC_W), f)
    ob = lambda f: pl.BlockSpec((None, None, R, LANES), f)
    i1 = lambda b, p, j: (p, b, j, 0)
    i4 = lambda b, p, j: (p, b, j % 4, j // 4)
    i16 = lambda b, p, j: (p, b, 0, j)
    outs = pl.pallas_call(
        _dil_kernel,
        grid=(B, 2, N_DIL_TILES),
        in_specs=[
            cur(i1),
            halo(lambda b, p, j: (p, b, jnp.maximum(j * hb - 1, 0), 0)),
            cur(i4),
            halo(lambda b, p, j: (p, b, jnp.maximum((j % 4) * hb - 1, 0), j // 4)),
            cur(i16),
        ],
        out_specs=[ob(i1), ob(i1), ob(i4), ob(i4), ob(i16), ob(i16)],
        out_shape=[
            jax.ShapeDtypeStruct((2, B, S, LANES), BF16),
            jax.ShapeDtypeStruct((2, B, S, LANES), F32),
            jax.ShapeDtypeStruct((2, B, S // 4, 4 * LANES), BF16),
            jax.ShapeDtypeStruct((2, B, S // 4, 4 * LANES), F32),
            jax.ShapeDtypeStruct((2, B, S // 16, 16 * LANES), BF16),
            jax.ShapeDtypeStruct((2, B, S // 16, 16 * LANES), F32),
        ],
        compiler_params=_cparams(("parallel", "parallel", "parallel")),
        name="dilated_mixer",
    )(v1, v1, v4, v4, v16)
    return [o.reshape(2, T, LANES) for o in outs]


def _outproj_kernel(x_ref, g1_ref, oa_ref, ob_ref, od_ref,
                    o1_ref, l1_ref, o4_ref, l4_ref, o16_ref, l16_ref, w_ref, xo_ref):
    oc = []
    for p in range(2):
        l1, l4, l16 = l1_ref[p], l4_ref[p], l16_ref[p]
        m = jnp.maximum(jnp.maximum(l1, l4), l16)
        e1, e4, e16 = jnp.exp(l1 - m), jnp.exp(l4 - m), jnp.exp(l16 - m)
        den = e1 + e4 + e16
        o = (e1 * o1_ref[p].astype(F32) + e4 * o4_ref[p].astype(F32) + e16 * o16_ref[p].astype(F32)) / den
        oc.append(o.astype(BF16))
    cat = jnp.concatenate([oa_ref[...], ob_ref[...], oc[0], oc[1], od_ref[...]], axis=1)
    xo_ref[...] = x_ref[...] + g1_ref[...] * _dot(cat, w_ref[...])


def _outproj(x2, mod3, oa, ob, od, dil_outs, w_out, S, tm):
    T, D = x2.shape
    nps = S // tm
    row = lambda w: pl.BlockSpec((tm, w), lambda i: (i, 0))
    pair = pl.BlockSpec((2, tm, LANES), lambda i: (0, i, 0))
    return pl.pallas_call(
        _outproj_kernel,
        grid=(T // tm,),
        in_specs=[row(D), pl.BlockSpec((None, 1, D), lambda i: (i // nps, 0, 2)),
                  row(GROUP_W), row(GROUP_W), row(GROUP_W)] + [pair] * 6
                 + [pl.BlockSpec((D, D), lambda i: (0, 0))],
        out_specs=row(D),
        out_shape=jax.ShapeDtypeStruct((T, D), F32),
        compiler_params=_cparams(("parallel",)),
        name="outproj",
    )(x2, mod3, oa, ob, od, *dil_outs, w_out)


FFN_HALO = 8
FFN_CHUNK = 256


def _ffn_kernel(x_ref, halo_ref, sh_ref, sc_ref, g2_ref, lng_ref, wup_ref, cw_ref, cb_ref,
                wdn_ref, lnf_ref, o_ref, act_ref, *, nps, final):
    tm = x_ref.shape[0]
    first = (pl.program_id(0) % nps) == 0
    x = x_ref[...]
    xe = jnp.concatenate([halo_ref[...], x], axis=0)
    h = _rms_mod(xe, lng_ref[...], sc_ref[...], sh_ref[...]).astype(BF16)
    rowid = lax.broadcasted_iota(jnp.int32, (tm + FFN_HALO, FFN_CHUNK), 0)
    pad = first & (rowid < FFN_HALO)

    def conv(cols):
        z = jnp.where(pad, 0.0, _dot(h, wup_ref[:, cols]))
        cw = cw_ref[:, cols]
        return (cw[2:3] * z[FFN_HALO:] + cw[1:2] * z[FFN_HALO - 1:tm + FFN_HALO - 1]
                + cw[0:1] * z[FFN_HALO - 2:tm + FFN_HALO - 2] + cb_ref[:, cols])

    for c in range(D_FF // FFN_CHUNK):
        a = conv(slice(c * FFN_CHUNK, (c + 1) * FFN_CHUNK))
        b = conv(slice(D_FF + c * FFN_CHUNK, D_FF + (c + 1) * FFN_CHUNK))
        act_ref[:, c * FFN_CHUNK:(c + 1) * FFN_CHUNK] = (_silu(a) * b).astype(BF16)
    y = x + g2_ref[...] * _dot(act_ref[...], wdn_ref[...])
    if final:
        ms = jnp.mean(y * y, axis=-1, keepdims=True)
        y = y * lax.rsqrt(ms + EPS) * lnf_ref[...]
    o_ref[...] = y


def _ffn(x2, mod3, ln_g, w_up, conv_w, conv_b, w_down, lnf_g, S, tm, final):
    T, D = x2.shape
    nps = S // tm
    hb = tm // FFN_HALO
    full = lambda *shape: pl.BlockSpec(shape, lambda i: (0,) * len(shape))
    modspec = lambda c: pl.BlockSpec((None, 1, D), lambda i: (i // nps, 0, c))
    return pl.pallas_call(
        functools.partial(_ffn_kernel, nps=nps, final=final),
        grid=(T // tm,),
        in_specs=[
            pl.BlockSpec((tm, D), lambda i: (i, 0)),
            pl.BlockSpec((FFN_HALO, D), lambda i: (jnp.maximum(i * hb - 1, 0), 0)),
            modspec(3), modspec(4), modspec(5),
            full(1, D),
            full(D, 2 * D_FF),
            full(FFN_CONV_K, 2 * D_FF),
            full(1, 2 * D_FF),
            full(D_FF, D),
            full(1, D),
        ],
        out_specs=pl.BlockSpec((tm, D), lambda i: (i, 0)),
        out_shape=jax.ShapeDtypeStruct((T, D), F32),
        scratch_shapes=[pltpu.VMEM((tm, D_FF), BF16)],
        compiler_params=_cparams(("parallel",), VMEM_LIMIT),
        name="conv_ffn",
    )(x2, x2, mod3, mod3, mod3, ln_g, w_up, conv_w, conv_b, w_down, lnf_g)


def _rearrange_w_in(w_in):
    offs = np.cumsum((0, 256, 256, 128, 128, 256, 256, 16, 256, 256, 256, 256, 256))
    a_val, a_gate, b_q, b_k, b_v, b_g, b_lr, c_q, c_k, c_v, d_u, d_v = [
        w_in[..., offs[i]:offs[i + 1]] for i in range(12)]
    lr_pad = jnp.zeros(w_in.shape[:-1] + (LANES - GLA_RANK,), w_in.dtype)
    cols = [a_val, a_gate, b_q, b_k, b_v, b_g, b_lr, lr_pad]
    for p in range(2):
        s = slice(p * LANES, (p + 1) * LANES)
        cols += [c_q[..., s], c_k[..., s], c_v[..., s]]
    cols += [d_u, d_v]
    return jnp.concatenate(cols, axis=-1).astype(BF16)


def _rope_tables(S):
    inv = 1.0 / (ROPE_THETA ** (jnp.arange(0, HEAD_DIM, 2, dtype=F32) / HEAD_DIM))
    ang = jnp.arange(S, dtype=F32)[:, None] * inv[None, :]
    cos, sin = jnp.cos(ang), jnp.sin(ang)
    cos_t = jnp.concatenate([cos, cos, cos, cos], axis=-1)
    sin_t = jnp.concatenate([-sin, sin, -sin, sin], axis=-1)
    return cos_t, sin_t


def kernel(x, c, ada_w, ada_b, ln1_g, w_in, conv_w, conv_b, cln_g, cln_b, gla_w2, gla_b, gla_norm_g, sgu_ln_g, sgu_ln_b, sgu_w, sgu_b, w_out, ln2_g, ffn_up, ffn_conv_w, ffn_conv_b, ffn_down, lnf_g):
    B, S, D = x.shape
    L = ada_w.shape[0]
    T = B * S
    assert D == D_MODEL and S % (N_DIL_TILES * DIL_BLOCK) == 0
    assert all(w // d == DIL_BLOCK for w, d in DIL_PAIRS)
    tm = 512
    t_ffn = 256
    t_conv = 256
    t_gla = 256

    mod = _modulation(c, ada_w, ada_b)
    wp = _rearrange_w_in(w_in)
    w_out_b = w_out.astype(BF16)
    w_up_b = ffn_up.astype(BF16)
    w_dn_b = ffn_down.astype(BF16)
    w2p = jnp.pad(gla_w2, ((0, 0), (0, LANES - GLA_RANK), (0, 0))).astype(BF16)
    cos_t, sin_t = _rope_tables(S)
    sgu_bias = jnp.repeat(jnp.swapaxes(sgu_b, 1, 2), HEAD_DIM, axis=2)
    ng = jnp.tile(gla_norm_g, (1, GROUP_HEADS))
    row = lambda a, l: a[l][None, :]

    x2 = x.reshape(T, D)
    for l in range(L):
        mod3 = mod[l].reshape(B, 1, 6 * D)
        ha, pb, pc, od = _inproj(x2, mod3, row(ln1_g, l), wp[l], cos_t, sin_t, sgu_w[l], sgu_bias[l],
                                 row(sgu_ln_g, l), row(sgu_ln_b, l), S, tm)
        oa = _conv_mixer(ha, conv_w[l], row(conv_b, l), row(cln_g, l), row(cln_b, l), S, t_conv)
        ob = _gla_mixer(pb, w2p[l], row(gla_b, l), row(ng, l), B, S, t_gla)
        dil_outs = _dilated_mixer(pc, B, S)
        x2 = _outproj(x2, mod3, oa, ob, od, dil_outs, w_out_b[l], S, tm)
        x2 = _ffn(x2, mod3, row(ln2_g, l), w_up_b[l], ffn_conv_w[l], row(ffn_conv_b, l), w_dn_b[l],
                  lnf_g[None, :], S, t_ffn, final=(l == L - 1))
    return x2.reshape(B, S, D)
```

```python
import functools

import numpy as np
import jax
import jax.numpy as jnp
from jax import lax
from jax.experimental import pallas as pl
from jax.experimental.pallas import tpu as pltpu

D_MODEL = 1024
GROUP_W = 256
GROUP_HEADS = 4
HEAD_DIM = 64
CONV_K = 31
GLA_DK = 32
GLA_RANK = 16
GLA_TAU = 16.0
GLA_CHUNK = 64
DIL_PAIRS = ((128, 1), (512, 4), (2048, 16))
DIL_BLOCK = 128
ROPE_THETA = 10000.0
SGU_CHUNK = 128
D_FF = 2816
FFN_CONV_K = 3
EPS = 1e-6
NEG_INF = -1e30

LANES = 128
BF16 = jnp.bfloat16
F32 = jnp.float32

SEC_A = (0, 512)
SEC_B = (512, 1408)
SEC_C = (1408, 2176)
SEC_D = (2176, 2688)
N_PROJ = 2688
PB_W = SEC_B[1] - SEC_B[0]
PC_W = 384

VMEM_LIMIT = 56 * 1024 * 1024


def _cparams(sem, vmem=None):
    return pltpu.CompilerParams(dimension_semantics=sem, vmem_limit_bytes=vmem)


def _dot(a, b):
    return jnp.dot(a, b, preferred_element_type=F32)


def _dot_nt(a, b):
    return lax.dot_general(a, b, (((1,), (1,)), ((), ())), preferred_element_type=F32)


def _dot_tn(a, b):
    return lax.dot_general(a, b, (((0,), (0,)), ((), ())), preferred_element_type=F32)


def _sigmoid(x):
    return 1.0 / (1.0 + jnp.exp(-x))


def _silu(x):
    return x * _sigmoid(x)


def _gelu_tanh(x):
    c = np.sqrt(2.0 / np.pi).astype(np.float32)
    return x * (0.5 * (1.0 + jnp.tanh(c * (x + 0.044715 * (x * x * x)))))


def _rms_mod(x, g, scale, shift):
    ms = jnp.mean(x * x, axis=-1, keepdims=True)
    y = x * lax.rsqrt(ms + EPS) * g
    return y * (1.0 + scale) + shift


def _layernorm(x, g, b):
    mu = jnp.mean(x, axis=-1, keepdims=True)
    xc = x - mu
    var = jnp.mean(xc * xc, axis=-1, keepdims=True)
    return xc * lax.rsqrt(var + EPS) * g + b


def _mod_kernel(c_ref, w_ref, b_ref, o_ref):
    cond = _silu(c_ref[...]).astype(BF16)
    o_ref[...] = _dot(cond, w_ref[...].astype(BF16)) + b_ref[...]


def _modulation(c, ada_w, ada_b):
    L, D, D6 = ada_w.shape
    B = c.shape[0]
    n = D6 // D
    return pl.pallas_call(
        _mod_kernel,
        grid=(L, n),
        in_specs=[
            pl.BlockSpec((B, D), lambda l, j: (0, 0)),
            pl.BlockSpec((None, D, D), lambda l, j: (l, 0, j)),
            pl.BlockSpec((None, 1, D), lambda l, j: (l, 0, j)),
        ],
        out_specs=pl.BlockSpec((None, B, D), lambda l, j: (l, 0, j)),
        out_shape=jax.ShapeDtypeStruct((L, B, D6), F32),
        compiler_params=_cparams(("parallel", "parallel")),
        name="adaln_mod",
    )(c, ada_w, ada_b.reshape(L, 1, D6))


def _inproj_kernel(x_ref, sh_ref, sc_ref, g_ref, w_ref, cos_ref, sin_ref,
                   sw_ref, sbias_ref, slg_ref, slb_ref,
                   ha_ref, pb_ref, pc_ref, od_ref):
    tm = x_ref.shape[0]
    h = _rms_mod(x_ref[...], g_ref[...], sc_ref[...], sh_ref[...]).astype(BF16)

    za = _dot(h, w_ref[:, SEC_A[0]:SEC_A[1]])
    ha_ref[...] = (za[:, :GROUP_W] * _sigmoid(za[:, GROUP_W:])).astype(BF16)

    pb_ref[...] = _dot(h, w_ref[:, SEC_B[0]:SEC_B[1]]).astype(BF16)

    zc = _dot(h, w_ref[:, SEC_C[0]:SEC_C[1]])
    cos_t = cos_ref[...]
    sin_t = sin_ref[...]
    lane = lax.broadcasted_iota(jnp.int32, (tm, LANES), 1)
    first_half = (lane % HEAD_DIM) < (HEAD_DIM // 2)

    def rope(t):
        partner = jnp.where(first_half, pltpu.roll(t, LANES - HEAD_DIM // 2, axis=1),
                            pltpu.roll(t, HEAD_DIM // 2, axis=1))
        return t * cos_t + partner * sin_t

    for p in range(2):
        base = p * PC_W
        q = rope(zc[:, base:base + LANES]) * (HEAD_DIM ** -0.5)
        k = rope(zc[:, base + LANES:base + 2 * LANES])
        v = zc[:, base + 2 * LANES:base + 3 * LANES]
        pc_ref[p, :, 0:LANES] = q.astype(BF16)
        pc_ref[p, :, LANES:2 * LANES] = k.astype(BF16)
        pc_ref[p, :, 2 * LANES:3 * LANES] = v.astype(BF16)

    zd = _dot(h, w_ref[:, SEC_D[0]:SEC_D[1]])
    u = _gelu_tanh(zd[:, :GROUP_W])
    vn = _layernorm(_gelu_tanh(zd[:, GROUP_W:]), slg_ref[...], slb_ref[...]).astype(BF16)
    row = lax.broadcasted_iota(jnp.int32, (SGU_CHUNK, SGU_CHUNK), 0)
    col = lax.broadcasted_iota(jnp.int32, (SGU_CHUNK, SGU_CHUNK), 1)
    causal = row >= col
    wm = [jnp.where(causal, sw_ref[g], 0.0).astype(BF16) for g in range(GROUP_HEADS)]
    grp = lax.broadcasted_iota(jnp.int32, (SGU_CHUNK, GROUP_W), 1) // HEAD_DIM
    sbias = sbias_ref[...]
    for c in range(tm // SGU_CHUNK):
        rows = slice(c * SGU_CHUNK, (c + 1) * SGU_CHUNK)
        vch = vn[rows]
        sv = sbias
        for g in range(GROUP_HEADS):
            sv = sv + jnp.where(grp == g, _dot(wm[g], vch), 0.0)
        od_ref[rows, :] = (u[rows] * sv).astype(BF16)


def _inproj(x2, mod3, ln_g, wp, cos_t, sin_t, sgu_w, sgu_bias, sgu_ln_g, sgu_ln_b, S, tm):
    T, D = x2.shape
    nps = S // tm
    full = lambda *shape: pl.BlockSpec(shape, lambda i: (0,) * len(shape))
    return pl.pallas_call(
        _inproj_kernel,
        grid=(T // tm,),
        in_specs=[
            pl.BlockSpec((tm, D), lambda i: (i, 0)),
            pl.BlockSpec((None, 1, D), lambda i: (i // nps, 0, 0)),
            pl.BlockSpec((None, 1, D), lambda i: (i // nps, 0, 1)),
            full(1, D),
            full(D, N_PROJ),
            pl.BlockSpec((tm, LANES), lambda i: (i % nps, 0)),
            pl.BlockSpec((tm, LANES), lambda i: (i % nps, 0)),
            full(GROUP_HEADS, SGU_CHUNK, SGU_CHUNK),
            full(SGU_CHUNK, GROUP_W),
            full(1, GROUP_W),
            full(1, GROUP_W),
        ],
        out_specs=[
            pl.BlockSpec((tm, GROUP_W), lambda i: (i, 0)),
            pl.BlockSpec((tm, PB_W), lambda i: (i, 0)),
            pl.BlockSpec((2, tm, PC_W), lambda i: (0, i, 0)),
            pl.BlockSpec((tm, GROUP_W), lambda i: (i, 0)),
        ],
        out_shape=[
            jax.ShapeDtypeStruct((T, GROUP_W), BF16),
            jax.ShapeDtypeStruct((T, PB_W), BF16),
            jax.ShapeDtypeStruct((2, T, PC_W), BF16),
            jax.ShapeDtypeStruct((T, GROUP_W), BF16),
        ],
        compiler_params=_cparams(("parallel",), VMEM_LIMIT),
        name="inproj",
    )(x2, mod3, mod3, ln_g, wp, cos_t, sin_t, sgu_w, sgu_bias, sgu_ln_g, sgu_ln_b)


CONV_HALO = 32
CONV_SUB = 64


def _conv_kernel(cur_ref, halo_ref, w_ref, b_ref, g_ref, beta_ref, o_ref, ext_ref, *, nps):
    tc = cur_ref.shape[0]
    first = (pl.program_id(0) % nps) == 0
    halo = halo_ref[...].astype(F32)
    ext_ref[0, 0:CONV_HALO, :] = jnp.where(first, 0.0, halo)
    ext_ref[0, CONV_HALO:, :] = cur_ref[...].astype(F32)
    n_shift = tc + CONV_HALO - 8
    for r in range(1, 8):
        ext_ref[r, 0:n_shift, :] = ext_ref[0, r:r + n_shift, :]
    off0 = CONV_HALO - (CONV_K - 1)
    for sub in range(tc // CONV_SUB):
        r0 = sub * CONV_SUB
        acc = jnp.zeros((CONV_SUB, GROUP_W), F32) + b_ref[...]
        for k in range(CONV_K):
            o = off0 + k
            a0 = r0 + 8 * (o // 8)
            acc = acc + w_ref[k:k + 1, :] * ext_ref[o % 8, a0:a0 + CONV_SUB, :]
        y = _layernorm(acc, g_ref[...], beta_ref[...])
        o_ref[r0:r0 + CONV_SUB, :] = _silu(y).astype(BF16)


def _conv_mixer(ha, conv_w, conv_b, ln_g, ln_b, S, tc):
    T = ha.shape[0]
    nps = S // tc
    hb = tc // CONV_HALO
    full = lambda *shape: pl.BlockSpec(shape, lambda i: (0,) * len(shape))
    return pl.pallas_call(
        functools.partial(_conv_kernel, nps=nps),
        grid=(T // tc,),
        in_specs=[
            pl.BlockSpec((tc, GROUP_W), lambda i: (i, 0)),
            pl.BlockSpec((CONV_HALO, GROUP_W), lambda i: (jnp.maximum(i * hb - 1, 0), 0)),
            full(CONV_K, GROUP_W),
            full(1, GROUP_W),
            full(1, GROUP_W),
            full(1, GROUP_W),
        ],
        out_specs=pl.BlockSpec((tc, GROUP_W), lambda i: (i, 0)),
        out_shape=jax.ShapeDtypeStruct((T, GROUP_W), BF16),
        scratch_shapes=[pltpu.VMEM((8, tc + CONV_HALO, GROUP_W), F32)],
        compiler_params=_cparams(("parallel",)),
        name="conv_mixer",
    )(ha, ha, conv_w, conv_b, ln_g, ln_b)


def _gla_kernel(pb_ref, w2_ref, gb_ref, ng_ref, o_ref, st_ref):
    tb = pb_ref.shape[0]
    C = GLA_CHUNK
    HK = GROUP_HEADS * GLA_DK

    @pl.when(pl.program_id(1) == 0)
    def _():
        st_ref[...] = jnp.zeros_like(st_ref)

    r_i = lax.broadcasted_iota(jnp.int32, (C, C), 0)
    c_i = lax.broadcasted_iota(jnp.int32, (C, C), 1)
    ltri = jnp.where(r_i >= c_i, 1.0, 0.0).astype(BF16)
    att_r = lax.broadcasted_iota(jnp.int32, (C, GROUP_HEADS * C), 0)
    att_c = lax.broadcasted_iota(jnp.int32, (C, GROUP_HEADS * C), 1) % C
    att_causal = att_r >= att_c
    ks_head_r = lax.broadcasted_iota(jnp.int32, (GROUP_HEADS * C, HK), 0) // C
    ks_head_c = lax.broadcasted_iota(jnp.int32, (GROUP_HEADS * C, HK), 1) // GLA_DK
    ks_mask = ks_head_r == ks_head_c
    vs_head_r = lax.broadcasted_iota(jnp.int32, (GROUP_HEADS * C, GROUP_W), 0) // C
    vs_head_c = lax.broadcasted_iota(jnp.int32, (GROUP_HEADS * C, GROUP_W), 1) // HEAD_DIM
    vs_mask = vs_head_r == vs_head_c
    st_head_r = lax.broadcasted_iota(jnp.int32, (GROUP_W, HK), 0) // HEAD_DIM
    st_head_c = lax.broadcasted_iota(jnp.int32, (GROUP_W, HK), 1) // GLA_DK
    st_mask = st_head_r == st_head_c
    o_head = lax.broadcasted_iota(jnp.int32, (C, GROUP_W), 1) // HEAD_DIM

    w2 = w2_ref[...]
    gb = gb_ref[...]
    ng = ng_ref[...]

    for ch in range(tb // C):
        rows = slice(ch * C, (ch + 1) * C)
        q = pb_ref[rows, 0:HK].astype(F32)
        k = pb_ref[rows, HK:2 * HK].astype(F32)
        v = pb_ref[rows, 2 * HK:2 * HK + GROUP_W]
        g = pb_ref[rows, 2 * HK + GROUP_W:2 * HK + 2 * GROUP_W].astype(F32)
        lr = pb_ref[rows, 2 * HK + 2 * GROUP_W:2 * HK + 2 * GROUP_W + LANES]

        pre = _dot(lr, w2) + gb
        gk = (jnp.minimum(pre, 0.0) - jnp.log1p(jnp.exp(-jnp.abs(pre)))) * (1.0 / GLA_TAU)
        g1 = gk.astype(BF16)
        rem = gk - g1.astype(F32)
        g2 = rem.astype(BF16)
        g3 = (rem - g2.astype(F32)).astype(BF16)
        b = _dot(ltri, g1) + _dot(ltri, g2) + _dot(ltri, g3)
        b_last = b[C - 1:C, :]
        q_t = (q * (GLA_DK ** -0.5) * jnp.exp(b)).astype(BF16)
        k_t = (k * jnp.exp(-b)).astype(BF16)
        k_end = (k * jnp.exp(b_last - b)).astype(BF16)

        zero_bf = jnp.zeros((), BF16)
        k_stack = jnp.where(ks_mask, jnp.concatenate([k_t] * GROUP_HEADS, axis=0), zero_bf)
        att = jnp.where(att_causal, _dot_nt(q_t, k_stack), 0.0).astype(BF16)
        v_stack = jnp.where(vs_mask, jnp.concatenate([v] * GROUP_HEADS, axis=0), zero_bf)
        o_intra = _dot(att, v_stack)

        st = st_ref[...]
        o_inter = _dot_nt(q_t, st.astype(BF16))
        upd = _dot_tn(v, k_end)
        st_ref[...] = st * jnp.exp(b_last) + jnp.where(st_mask, upd, 0.0)

        o = o_intra + o_inter
        o2 = o * o
        inv = jnp.zeros((C, GROUP_W), F32)
        for hh in range(GROUP_HEADS):
            msq = jnp.sum(jnp.where(o_head == hh, o2, 0.0), axis=-1, keepdims=True) * (1.0 / HEAD_DIM)
            inv = jnp.where(o_head == hh, lax.rsqrt(msq + EPS), inv)
        o_ref[rows, :] = (o * inv * ng * _silu(g)).astype(BF16)


def _gla_mixer(pb, w2p, gb, ng, B, S, tb):
    T = pb.shape[0]
    nps = S // tb
    full = lambda *shape: pl.BlockSpec(shape, lambda b, i: (0,) * len(shape))
    return pl.pallas_call(
        _gla_kernel,
        grid=(B, nps),
        in_specs=[
            pl.BlockSpec((tb, PB_W), lambda b, i: (b * nps + i, 0)),
            full(LANES, LANES),
            full(1, LANES),
            full(1, GROUP_W),
        ],
        out_specs=pl.BlockSpec((tb, GROUP_W), lambda b, i: (b * nps + i, 0)),
        out_shape=jax.ShapeDtypeStruct((T, GROUP_W), BF16),
        scratch_shapes=[pltpu.VMEM((GROUP_W, GROUP_HEADS * GLA_DK), F32)],
        compiler_params=_cparams(("arbitrary", "arbitrary")),
        name="gla_mixer",
    )(pb, w2p, gb, ng)


PERM = 256
DIL_UNROLL = 4


def _perm_matrix(dil, transpose=False):
    i = lax.broadcasted_iota(jnp.int32, (PERM, PERM), 0)
    j = lax.broadcasted_iota(jnp.int32, (PERM, PERM), 1)
    if transpose:
        i, j = j, i
    g = PERM // dil
    return jnp.where(j == (i % g) * dil + i // g, 1.0, 0.0).astype(BF16)


def _dil_kernel(pc_ref, o_ref, p4_ref, p16_ref, ro4_ref, rl4_ref, ro16_ref, rl16_ref):
    S = pc_ref.shape[0]
    BLK = DIL_BLOCK
    nblk = S // BLK
    qi = lax.broadcasted_iota(jnp.int32, (2 * BLK, 2 * BLK), 0) % BLK
    kj = lax.broadcasted_iota(jnp.int32, (2 * BLK, 2 * BLK), 1)
    band = (kj >= qi) & (kj <= qi + BLK)
    lane = lax.broadcasted_iota(jnp.int32, (BLK, LANES), 1)
    head0 = lane < HEAD_DIM
    zero_bf = jnp.zeros((), BF16)
    strided = ((4, p4_ref, ro4_ref, rl4_ref), (16, p16_ref, ro16_ref, rl16_ref))

    def attend(src, blk):
        row0 = pl.multiple_of(blk * BLK, BLK)
        prow0 = pl.multiple_of(jnp.maximum(blk - 1, 0) * BLK, BLK)
        q = src[pl.ds(row0, BLK), 0:LANES]
        qs = jnp.concatenate([jnp.where(head0, q, zero_bf), jnp.where(head0, zero_bf, q)], axis=0)
        kk = jnp.concatenate([src[pl.ds(prow0, BLK), LANES:2 * LANES],
                              src[pl.ds(row0, BLK), LANES:2 * LANES]], axis=0)
        vv = jnp.concatenate([src[pl.ds(prow0, BLK), 2 * LANES:3 * LANES],
                              src[pl.ds(row0, BLK), 2 * LANES:3 * LANES]], axis=0)
        mask = band & (kj >= jnp.where(blk == 0, BLK, 0))
        s = jnp.where(mask, _dot_nt(qs, kk), NEG_INF)
        m = jnp.max(s, axis=-1, keepdims=True)
        p = jnp.exp(s - m)
        den = jnp.sum(p, axis=-1, keepdims=True)
        on = _dot(p.astype(BF16), vv) / den
        lse = m + jnp.log(den)
        return (jnp.where(head0, on[:BLK], on[BLK:]).astype(BF16),
                jnp.where(head0, lse[:BLK], lse[BLK:]))

    perms = [_perm_matrix(d) for d, _, _, _ in strided]

    def perm_body(it, carry):
        for u in range(2):
            blk = it * 2 + u
            xblk = pc_ref[pl.ds(pl.multiple_of(blk * PERM, PERM), PERM), :]
            for (dil, dst, _, _), pm in zip(strided, perms):
                g = PERM // dil
                y = _dot(pm, xblk).astype(BF16)
                g0 = pl.multiple_of(blk * g, g)
                for r in range(dil):
                    dst[r, pl.ds(g0, g), :] = y[r * g:(r + 1) * g]
        return carry

    lax.fori_loop(0, S // PERM // 2, perm_body, 0)

    for dil, src, ro, rl in strided:
        nb = nblk // dil

        def body(it, carry, nb=nb, src=src, ro=ro, rl=rl):
            for u in range(DIL_UNROLL):
                flat = it * DIL_UNROLL + u
                r = flat // nb
                blk = flat % nb
                o, l = attend(src.at[r], blk)
                row0 = pl.multiple_of(blk * BLK, BLK)
                ro[r, pl.ds(row0, BLK), :] = o
                rl[r, pl.ds(row0, BLK), :] = l
            return carry

        lax.fori_loop(0, nblk // DIL_UNROLL, body, 0)

    unperms = [_perm_matrix(d, transpose=True) for d, _, _, _ in strided]

    def comb_body(it, carry):
        for u in range(2):
            blk = it * 2 + u
            outs, lses = [], []
            for h in range(PERM // BLK):
                o, l = attend(pc_ref, blk * (PERM // BLK) + h)
                outs.append(o)
                lses.append(l)
            branches = [(jnp.concatenate(outs, axis=0).astype(F32), jnp.concatenate(lses, axis=0))]
            for (dil, _, ro, rl), um in zip(strided, unperms):
                g = PERM // dil
                g0 = pl.multiple_of(blk * g, g)
                go = jnp.concatenate([ro[r, pl.ds(g0, g), :] for r in range(dil)], axis=0)
                gl = jnp.concatenate([rl[r, pl.ds(g0, g), :] for r in range(dil)], axis=0)
                hi = gl.astype(BF16)
                rem = gl - hi.astype(F32)
                mid = rem.astype(BF16)
                lo = (rem - mid.astype(F32)).astype(BF16)
                a = _dot(um, jnp.concatenate([go, hi], axis=1))
                b = _dot(um, jnp.concatenate([mid, lo], axis=1))
                branches.append((a[:, :LANES], (a[:, LANES:] + b[:, :LANES]) + b[:, LANES:]))
            m = jnp.maximum(jnp.maximum(branches[0][1], branches[1][1]), branches[2][1])
            es = [jnp.exp(l - m) for _, l in branches]
            den = es[0] + es[1] + es[2]
            num = es[0] * branches[0][0] + es[1] * branches[1][0] + es[2] * branches[2][0]
            o_ref[pl.ds(pl.multiple_of(blk * PERM, PERM), PERM), :] = (num / den).astype(BF16)
        return carry

    lax.fori_loop(0, S // PERM // 2, comb_body, 0)


def _dilated_mixer(pc, B, S):
    v1 = pc.reshape(2, B, S, PC_W)
    oc = pl.pallas_call(
        _dil_kernel,
        grid=(B, 2),
        in_specs=[pl.BlockSpec((None, None, S, PC_W), lambda b, p: (p, b, 0, 0))],
        out_specs=pl.BlockSpec((None, S, LANES), lambda b, p: (b, 0, p)),
        out_shape=jax.ShapeDtypeStruct((B, S, GROUP_W), BF16),
        scratch_shapes=[
            pltpu.VMEM((4, S // 4, PC_W), BF16),
            pltpu.VMEM((16, S // 16, PC_W), BF16),
            pltpu.VMEM((4, S // 4, LANES), BF16),
            pltpu.VMEM((4, S // 4, LANES), F32),
            pltpu.VMEM((16, S // 16, LANES), BF16),
            pltpu.VMEM((16, S // 16, LANES), F32),
        ],
        compiler_params=_cparams(("parallel", "parallel"), VMEM_LIMIT),
        name="dilated_mixer",
    )(v1)
    return oc.reshape(B * S, GROUP_W)


def _outproj_kernel(x_ref, g1_ref, oa_ref, ob_ref, oc_ref, od_ref, w_ref, xo_ref):
    cat = jnp.concatenate([oa_ref[...], ob_ref[...], oc_ref[...], od_ref[...]], axis=1)
    xo_ref[...] = x_ref[...] + g1_ref[...] * _dot(cat, w_ref[...])


def _outproj(x2, mod3, oa, ob, oc, od, w_out, S, tm):
    T, D = x2.shape
    nps = S // tm
    row = lambda w: pl.BlockSpec((tm, w), lambda i: (i, 0))
    return pl.pallas_call(
        _outproj_kernel,
        grid=(T // tm,),
        in_specs=[row(D), pl.BlockSpec((None, 1, D), lambda i: (i // nps, 0, 2)),
                  row(GROUP_W), row(GROUP_W), row(GROUP_W), row(GROUP_W),
                  pl.BlockSpec((D, D), lambda i: (0, 0))],
        out_specs=row(D),
        out_shape=jax.ShapeDtypeStruct((T, D), F32),
        compiler_params=_cparams(("parallel",)),
        name="outproj",
    )(x2, mod3, oa, ob, oc, od, w_out)


FFN_HALO = 8
FFN_CHUNK = 256


def _ffn_kernel(x_ref, halo_ref, sh_ref, sc_ref, g2_ref, lng_ref, wup_ref, cw_ref, cb_ref,
                wdn_ref, lnf_ref, o_ref, act_ref, *, nps, final):
    tm = x_ref.shape[0]
    first = (pl.program_id(0) % nps) == 0
    x = x_ref[...]
    xe = jnp.concatenate([halo_ref[...], x], axis=0)
    h = _rms_mod(xe, lng_ref[...], sc_ref[...], sh_ref[...]).astype(BF16)
    rowid = lax.broadcasted_iota(jnp.int32, (tm + FFN_HALO, FFN_CHUNK), 0)
    pad = first & (rowid < FFN_HALO)

    def conv(cols):
        z = jnp.where(pad, 0.0, _dot(h, wup_ref[:, cols]))
        cw = cw_ref[:, cols]
        return (cw[2:3] * z[FFN_HALO:] + cw[1:2] * z[FFN_HALO - 1:tm + FFN_HALO - 1]
                + cw[0:1] * z[FFN_HALO - 2:tm + FFN_HALO - 2] + cb_ref[:, cols])

    for c in range(D_FF // FFN_CHUNK):
        a = conv(slice(c * FFN_CHUNK, (c + 1) * FFN_CHUNK))
        b = conv(slice(D_FF + c * FFN_CHUNK, D_FF + (c + 1) * FFN_CHUNK))
        act_ref[:, c * FFN_CHUNK:(c + 1) * FFN_CHUNK] = (_silu(a) * b).astype(BF16)
    y = x + g2_ref[...] * _dot(act_ref[...], wdn_ref[...])
    if final:
        ms = jnp.mean(y * y, axis=-1, keepdims=True)
        y = y * lax.rsqrt(ms + EPS) * lnf_ref[...]
    o_ref[...] = y


def _ffn(x2, mod3, ln_g, w_up, conv_w, conv_b, w_down, lnf_g, S, tm, final):
    T, D = x2.shape
    nps = S // tm
    hb = tm // FFN_HALO
    full = lambda *shape: pl.BlockSpec(shape, lambda i: (0,) * len(shape))
    modspec = lambda c: pl.BlockSpec((None, 1, D), lambda i: (i // nps, 0, c))
    return pl.pallas_call(
        functools.partial(_ffn_kernel, nps=nps, final=final),
        grid=(T // tm,),
        in_specs=[
            pl.BlockSpec((tm, D), lambda i: (i, 0)),
            pl.BlockSpec((FFN_HALO, D), lambda i: (jnp.maximum(i * hb - 1, 0), 0)),
            modspec(3), modspec(4), modspec(5),
            full(1, D),
            full(D, 2 * D_FF),
            full(FFN_CONV_K, 2 * D_FF),
            full(1, 2 * D_FF),
            full(D_FF, D),
            full(1, D),
        ],
        out_specs=pl.BlockSpec((tm, D), lambda i: (i, 0)),
        out_shape=jax.ShapeDtypeStruct((T, D), F32),
        scratch_shapes=[pltpu.VMEM((tm, D_FF), BF16)],
        compiler_params=_cparams(("parallel",), VMEM_LIMIT),
        name="conv_ffn",
    )(x2, x2, mod3, mod3, mod3, ln_g, w_up, conv_w, conv_b, w_down, lnf_g)


def _rearrange_w_in(w_in):
    offs = np.cumsum((0, 256, 256, 128, 128, 256, 256, 16, 256, 256, 256, 256, 256))
    a_val, a_gate, b_q, b_k, b_v, b_g, b_lr, c_q, c_k, c_v, d_u, d_v = [
        w_in[..., offs[i]:offs[i + 1]] for i in range(12)]
    lr_pad = jnp.zeros(w_in.shape[:-1] + (LANES - GLA_RANK,), w_in.dtype)
    cols = [a_val, a_gate, b_q, b_k, b_v, b_g, b_lr, lr_pad]
    for p in range(2):
        s = slice(p * LANES, (p + 1) * LANES)
        cols += [c_q[..., s], c_k[..., s], c_v[..., s]]
    cols += [d_u, d_v]
    return jnp.concatenate(cols, axis=-1).astype(BF16)


def _rope_tables(S):
    inv = 1.0 / (ROPE_THETA ** (jnp.arange(0, HEAD_DIM, 2, dtype=F32) / HEAD_DIM))
    ang = jnp.arange(S, dtype=F32)[:, None] * inv[None, :]
    cos, sin = jnp.cos(ang), jnp.sin(ang)
    cos_t = jnp.concatenate([cos, cos, cos, cos], axis=-1)
    sin_t = jnp.concatenate([-sin, sin, -sin, sin], axis=-1)
    return cos_t, sin_t


def kernel(x, c, ada_w, ada_b, ln1_g, w_in, conv_w, conv_b, cln_g, cln_b, gla_w2, gla_b, gla_norm_g, sgu_ln_g, sgu_ln_b, sgu_w, sgu_b, w_out, ln2_g, ffn_up, ffn_conv_w, ffn_conv_b, ffn_down, lnf_g):
    B, S, D = x.shape
    L = ada_w.shape[0]
    T = B * S
    assert D == D_MODEL and S % (16 * DIL_BLOCK) == 0
    assert all(w // d == DIL_BLOCK for w, d in DIL_PAIRS)
    tm = 512
    t_ffn = 256
    t_conv = 256
    t_gla = 256

    mod = _modulation(c, ada_w, ada_b)
    wp = _rearrange_w_in(w_in)
    w_out_b = w_out.astype(BF16)
    w_up_b = ffn_up.astype(BF16)
    w_dn_b = ffn_down.astype(BF16)
    w2p = jnp.pad(gla_w2, ((0, 0), (0, LANES - GLA_RANK), (0, 0))).astype(BF16)
    cos_t, sin_t = _rope_tables(S)
    sgu_bias = jnp.repeat(jnp.swapaxes(sgu_b, 1, 2), HEAD_DIM, axis=2)
    ng = jnp.tile(gla_norm_g, (1, GROUP_HEADS))
    row = lambda a, l: a[l][None, :]

    x2 = x.reshape(T, D)
    for l in range(L):
        mod3 = mod[l].reshape(B, 1, 6 * D)
        ha, pb, pc, od = _inproj(x2, mod3, row(ln1_g, l), wp[l], cos_t, sin_t, sgu_w[l], sgu_bias[l],
                                 row(sgu_ln_g, l), row(sgu_ln_b, l), S, tm)
        oa = _conv_mixer(ha, conv_w[l], row(conv_b, l), row(cln_g, l), row(cln_b, l), S, t_conv)
        ob = _gla_mixer(pb, w2p[l], row(gla_b, l), row(ng, l), B, S, t_gla)
        oc = _dilated_mixer(pc, B, S)
        x2 = _outproj(x2, mod3, oa, ob, oc, od, w_out_b[l], S, tm)
        x2 = _ffn(x2, mod3, row(ln2_g, l), w_up_b[l], ffn_conv_w[l], row(ffn_conv_b, l), w_dn_b[l],
                  lnf_g[None, :], S, t_ffn, final=(l == L - 1))
    return x2.reshape(B, S, D)
```

```python
import functools

import numpy as np
import jax
import jax.numpy as jnp
from jax import lax
from jax.experimental import pallas as pl
from jax.experimental.pallas import tpu as pltpu

D_MODEL = 1024
GROUP_W = 256
GROUP_HEADS = 4
HEAD_DIM = 64
CONV_K = 31
GLA_DK = 32
GLA_RANK = 16
GLA_TAU = 16.0
GLA_CHUNK = 64
DIL_PAIRS = ((128, 1), (512, 4), (2048, 16))
DIL_BLOCK = 128
ROPE_THETA = 10000.0
SGU_CHUNK = 128
D_FF = 2816
FFN_CONV_K = 3
EPS = 1e-6
NEG_INF = -1e30

LANES = 128
BF16 = jnp.bfloat16
F32 = jnp.float32

SEC_A = (0, 512)
SEC_B = (512, 1408)
SEC_C = (1408, 2176)
SEC_D = (2176, 2688)
N_PROJ = 2688
PB_W = SEC_B[1] - SEC_B[0]
PC_W = 384
INPROJ_SUB = 256

VMEM_LIMIT = 56 * 1024 * 1024


def _cparams(sem, vmem=None):
    return pltpu.CompilerParams(dimension_semantics=sem, vmem_limit_bytes=vmem)


def _dot(a, b):
    return jnp.dot(a, b, preferred_element_type=F32)


def _dot_nt(a, b):
    return lax.dot_general(a, b, (((1,), (1,)), ((), ())), preferred_element_type=F32)


def _dot_tn(a, b):
    return lax.dot_general(a, b, (((0,), (0,)), ((), ())), preferred_element_type=F32)


def _sigmoid(x):
    return 1.0 / (1.0 + jnp.exp(-x))


def _silu(x):
    return x * _sigmoid(x)


def _gelu_tanh(x):
    c = np.sqrt(2.0 / np.pi).astype(np.float32)
    return x * (0.5 * (1.0 + jnp.tanh(c * (x + 0.044715 * (x * x * x)))))


def _rms_mod(x, g, scale, shift):
    ms = jnp.mean(x * x, axis=-1, keepdims=True)
    y = x * lax.rsqrt(ms + EPS) * g
    return y * (1.0 + scale) + shift


def _layernorm(x, g, b):
    mu = jnp.mean(x, axis=-1, keepdims=True)
    xc = x - mu
    var = jnp.mean(xc * xc, axis=-1, keepdims=True)
    return xc * lax.rsqrt(var + EPS) * g + b


def _mod_kernel(c_ref, w_ref, b_ref, o_ref):
    cond = _silu(c_ref[...]).astype(BF16)
    o_ref[...] = _dot(cond, w_ref[...].astype(BF16)) + b_ref[...]


def _modulation(c, ada_w, ada_b):
    L, D, D6 = ada_w.shape
    B = c.shape[0]
    n = D6 // D
    return pl.pallas_call(
        _mod_kernel,
        grid=(L, n),
        in_specs=[
            pl.BlockSpec((B, D), lambda l, j: (0, 0)),
            pl.BlockSpec((None, D, D), lambda l, j: (l, 0, j)),
            pl.BlockSpec((None, 1, D), lambda l, j: (l, 0, j)),
        ],
        out_specs=pl.BlockSpec((None, B, D), lambda l, j: (l, 0, j)),
        out_shape=jax.ShapeDtypeStruct((L, B, D6), F32),
        compiler_params=_cparams(("parallel", "parallel")),
        name="adaln_mod",
    )(c, ada_w, ada_b.reshape(L, 1, D6))


def _inproj_kernel(x_ref, sh_ref, sc_ref, g_ref, w_ref, cos_ref, sin_ref,
                   sw_ref, sbias_ref, slg_ref, slb_ref,
                   ha_ref, pb_ref, pc_ref, od_ref):
    tm = x_ref.shape[0]
    ts = INPROJ_SUB
    lane = lax.broadcasted_iota(jnp.int32, (ts, LANES), 1)
    first_half = (lane % HEAD_DIM) < (HEAD_DIM // 2)
    row = lax.broadcasted_iota(jnp.int32, (SGU_CHUNK, SGU_CHUNK), 0)
    col = lax.broadcasted_iota(jnp.int32, (SGU_CHUNK, SGU_CHUNK), 1)
    causal = row >= col
    wm = [jnp.where(causal, sw_ref[g], 0.0).astype(BF16) for g in range(GROUP_HEADS)]
    grp = lax.broadcasted_iota(jnp.int32, (SGU_CHUNK, GROUP_W), 1) // HEAD_DIM
    sbias = sbias_ref[...]

    for sub in range(tm // ts):
        rs = slice(sub * ts, (sub + 1) * ts)
        h = _rms_mod(x_ref[rs, :], g_ref[...], sc_ref[...], sh_ref[...]).astype(BF16)

        za = _dot(h, w_ref[:, SEC_A[0]:SEC_A[1]])
        ha_ref[rs, :] = (za[:, :GROUP_W] * _sigmoid(za[:, GROUP_W:])).astype(BF16)

        pb_ref[rs, :] = _dot(h, w_ref[:, SEC_B[0]:SEC_B[1]]).astype(BF16)

        zc = _dot(h, w_ref[:, SEC_C[0]:SEC_C[1]])
        cos_t = cos_ref[rs, :]
        sin_t = sin_ref[rs, :]

        def rope(t):
            partner = jnp.where(first_half, pltpu.roll(t, LANES - HEAD_DIM // 2, axis=1),
                                pltpu.roll(t, HEAD_DIM // 2, axis=1))
            return t * cos_t + partner * sin_t

        for p in range(2):
            base = p * PC_W
            q = rope(zc[:, base:base + LANES]) * (HEAD_DIM ** -0.5)
            k = rope(zc[:, base + LANES:base + 2 * LANES])
            v = zc[:, base + 2 * LANES:base + 3 * LANES]
            pc_ref[p, rs, 0:LANES] = q.astype(BF16)
            pc_ref[p, rs, LANES:2 * LANES] = k.astype(BF16)
            pc_ref[p, rs, 2 * LANES:3 * LANES] = v.astype(BF16)

        zd = _dot(h, w_ref[:, SEC_D[0]:SEC_D[1]])
        u = _gelu_tanh(zd[:, :GROUP_W])
        vn = _layernorm(_gelu_tanh(zd[:, GROUP_W:]), slg_ref[...], slb_ref[...]).astype(BF16)
        for c in range(ts // SGU_CHUNK):
            rows = slice(c * SGU_CHUNK, (c + 1) * SGU_CHUNK)
            vch = vn[rows]
            sv = sbias
            for g in range(GROUP_HEADS):
                sv = sv + jnp.where(grp == g, _dot(wm[g], vch), 0.0)
            od_ref[sub * ts + c * SGU_CHUNK:sub * ts + (c + 1) * SGU_CHUNK, :] = (u[rows] * sv).astype(BF16)


def _inproj(x2, mod3, ln_g, wp, cos_t, sin_t, sgu_w, sgu_bias, sgu_ln_g, sgu_ln_b, S, tm):
    T, D = x2.shape
    nps = S // tm
    full = lambda *shape: pl.BlockSpec(shape, lambda i: (0,) * len(shape))
    return pl.pallas_call(
        _inproj_kernel,
        grid=(T // tm,),
        in_specs=[
            pl.BlockSpec((tm, D), lambda i: (i, 0)),
            pl.BlockSpec((None, 1, D), lambda i: (i // nps, 0, 0)),
            pl.BlockSpec((None, 1, D), lambda i: (i // nps, 0, 1)),
            full(1, D),
            full(D, N_PROJ),
            pl.BlockSpec((tm, LANES), lambda i: (i % nps, 0)),
            pl.BlockSpec((tm, LANES), lambda i: (i % nps, 0)),
            full(GROUP_HEADS, SGU_CHUNK, SGU_CHUNK),
            full(SGU_CHUNK, GROUP_W),
            full(1, GROUP_W),
            full(1, GROUP_W),
        ],
        out_specs=[
            pl.BlockSpec((tm, GROUP_W), lambda i: (i, 0)),
            pl.BlockSpec((tm, PB_W), lambda i: (i, 0)),
            pl.BlockSpec((2, tm, PC_W), lambda i: (0, i, 0)),
            pl.BlockSpec((tm, GROUP_W), lambda i: (i, 0)),
        ],
        out_shape=[
            jax.ShapeDtypeStruct((T, GROUP_W), BF16),
            jax.ShapeDtypeStruct((T, PB_W), BF16),
            jax.ShapeDtypeStruct((2, T, PC_W), BF16),
            jax.ShapeDtypeStruct((T, GROUP_W), BF16),
        ],
        compiler_params=_cparams(("parallel",), VMEM_LIMIT),
        name="inproj",
    )(x2, mod3, mod3, ln_g, wp, cos_t, sin_t, sgu_w, sgu_bias, sgu_ln_g, sgu_ln_b)


CONV_HALO = 32
CONV_SUB = 64


def _conv_kernel(cur_ref, halo_ref, w_ref, b_ref, g_ref, beta_ref, o_ref, ext_ref, *, nps):
    tc = cur_ref.shape[0]
    first = (pl.program_id(0) % nps) == 0
    halo = halo_ref[...].astype(F32)
    ext_ref[0, 0:CONV_HALO, :] = jnp.where(first, 0.0, halo)
    ext_ref[0, CONV_HALO:, :] = cur_ref[...].astype(F32)
    n_shift = tc + CONV_HALO - 8
    for r in range(1, 8):
        ext_ref[r, 0:n_shift, :] = ext_ref[0, r:r + n_shift, :]
    off0 = CONV_HALO - (CONV_K - 1)
    for sub in range(tc // CONV_SUB):
        r0 = sub * CONV_SUB
        acc = jnp.zeros((CONV_SUB, GROUP_W), F32) + b_ref[...]
        for k in range(CONV_K):
            o = off0 + k
            a0 = r0 + 8 * (o // 8)
            acc = acc + w_ref[k:k + 1, :] * ext_ref[o % 8, a0:a0 + CONV_SUB, :]
        y = _layernorm(acc, g_ref[...], beta_ref[...])
        o_ref[r0:r0 + CONV_SUB, :] = _silu(y).astype(BF16)


def _conv_mixer(ha, conv_w, conv_b, ln_g, ln_b, S, tc):
    T = ha.shape[0]
    nps = S // tc
    hb = tc // CONV_HALO
    full = lambda *shape: pl.BlockSpec(shape, lambda i: (0,) * len(shape))
    return pl.pallas_call(
        functools.partial(_conv_kernel, nps=nps),
        grid=(T // tc,),
        in_specs=[
            pl.BlockSpec((tc, GROUP_W), lambda i: (i, 0)),
            pl.BlockSpec((CONV_HALO, GROUP_W), lambda i: (jnp.maximum(i * hb - 1, 0), 0)),
            full(CONV_K, GROUP_W),
            full(1, GROUP_W),
            full(1, GROUP_W),
            full(1, GROUP_W),
        ],
        out_specs=pl.BlockSpec((tc, GROUP_W), lambda i: (i, 0)),
        out_shape=jax.ShapeDtypeStruct((T, GROUP_W), BF16),
        scratch_shapes=[pltpu.VMEM((8, tc + CONV_HALO, GROUP_W), F32)],
        compiler_params=_cparams(("parallel",)),
        name="conv_mixer",
    )(ha, ha, conv_w, conv_b, ln_g, ln_b)


def _gla_kernel(pb_ref, w2_ref, gb_ref, ng_ref, o_ref, st_ref):
    nbatch, tb = pb_ref.shape[0], pb_ref.shape[1]
    C = GLA_CHUNK
    HK = GROUP_HEADS * GLA_DK

    @pl.when(pl.program_id(1) == 0)
    def _():
        st_ref[...] = jnp.zeros_like(st_ref)

    r_i = lax.broadcasted_iota(jnp.int32, (tb, tb), 0)
    c_i = lax.broadcasted_iota(jnp.int32, (tb, tb), 1)
    ltri = jnp.where((r_i >= c_i) & (r_i // C == c_i // C), 1.0, 0.0).astype(BF16)
    att_r = lax.broadcasted_iota(jnp.int32, (C, GROUP_HEADS * C), 0)
    att_c = lax.broadcasted_iota(jnp.int32, (C, GROUP_HEADS * C), 1) % C
    att_causal = att_r >= att_c
    ks_mask = (lax.broadcasted_iota(jnp.int32, (GROUP_HEADS * C, HK), 0) // C
               == lax.broadcasted_iota(jnp.int32, (GROUP_HEADS * C, HK), 1) // GLA_DK)
    vs_mask = (lax.broadcasted_iota(jnp.int32, (GROUP_HEADS * C, GROUP_W), 0) // C
               == lax.broadcasted_iota(jnp.int32, (GROUP_HEADS * C, GROUP_W), 1) // HEAD_DIM)
    st_mask = (lax.broadcasted_iota(jnp.int32, (HK, GROUP_W), 0) // GLA_DK
               == lax.broadcasted_iota(jnp.int32, (HK, GROUP_W), 1) // HEAD_DIM)
    low_half = lax.broadcasted_iota(jnp.int32, (C, LANES), 1) < HEAD_DIM
    zero_bf = jnp.zeros((), BF16)

    w2 = w2_ref[...]
    gb = gb_ref[...]
    ng = ng_ref[...]

    for bb in range(nbatch):
        q = pb_ref[bb, :, 0:HK].astype(F32)
        k = pb_ref[bb, :, HK:2 * HK].astype(F32)
        lr = pb_ref[bb, :, 2 * HK + 2 * GROUP_W:2 * HK + 2 * GROUP_W + LANES]
        pre = _dot(lr, w2) + gb
        gk = (jnp.minimum(pre, 0.0) - jnp.log(1.0 + jnp.exp(-jnp.abs(pre)))) * (1.0 / GLA_TAU)
        g1 = gk.astype(BF16)
        rem = gk - g1.astype(F32)
        g2 = rem.astype(BF16)
        g3 = (rem - g2.astype(F32)).astype(BF16)
        b = _dot(ltri, g1) + _dot(ltri, g2) + _dot(ltri, g3)
        q_t = (q * (GLA_DK ** -0.5) * jnp.exp(b)).astype(BF16)
        k_t = (k * jnp.exp(-b)).astype(BF16)
        st = st_ref[bb]
        for ch in range(tb // C):
            rows = slice(ch * C, (ch + 1) * C)
            v = pb_ref[bb, rows, 2 * HK:2 * HK + GROUP_W]
            g = pb_ref[bb, rows, 2 * HK + GROUP_W:2 * HK + 2 * GROUP_W].astype(F32)
            b_c = b[rows]
            b_last = b_c[C - 1:C, :]
            k_end = (k[rows] * jnp.exp(b_last - b_c)).astype(BF16)
            q_c = q_t[rows]
            k_stack = jnp.where(ks_mask, jnp.concatenate([k_t[rows]] * GROUP_HEADS, axis=0), zero_bf)
            att = jnp.where(att_causal, _dot_nt(q_c, k_stack), 0.0).astype(BF16)
            v_stack = jnp.where(vs_mask, jnp.concatenate([v] * GROUP_HEADS, axis=0), zero_bf)
            o = _dot(att, v_stack) + _dot(q_c, st.astype(BF16))
            dec_col = jnp.exp(jnp.transpose(b_c[C - 8:C, :])[:, 7:8])
            st = st * dec_col + jnp.where(st_mask, _dot_tn(k_end, v), 0.0)
            invs = []
            for grp in range(GROUP_W // LANES):
                og = o[:, grp * LANES:(grp + 1) * LANES]
                o2 = og * og
                lo = jnp.sum(jnp.where(low_half, o2, 0.0), axis=-1, keepdims=True) * (1.0 / HEAD_DIM)
                hi = jnp.sum(jnp.where(low_half, 0.0, o2), axis=-1, keepdims=True) * (1.0 / HEAD_DIM)
                invs.append(jnp.where(low_half, lax.rsqrt(lo + EPS), lax.rsqrt(hi + EPS)))
            inv = jnp.concatenate(invs, axis=1)
            o_ref[bb, rows, :] = (o * inv * ng * _silu(g)).astype(BF16)
        st_ref[bb] = st


def _gla_mixer(pb, w2p, gb, ng, B, S, tb, nbatch):
    T = pb.shape[0]
    nps = S // tb
    full = lambda *shape: pl.BlockSpec(shape, lambda b, i: (0,) * len(shape))
    out = pl.pallas_call(
        _gla_kernel,
        grid=(B // nbatch, nps),
        in_specs=[
            pl.BlockSpec((nbatch, tb, PB_W), lambda b, i: (b, i, 0)),
            full(LANES, LANES),
            full(1, LANES),
            full(1, GROUP_W),
        ],
        out_specs=pl.BlockSpec((nbatch, tb, GROUP_W), lambda b, i: (b, i, 0)),
        out_shape=jax.ShapeDtypeStruct((B, S, GROUP_W), BF16),
        scratch_shapes=[pltpu.VMEM((nbatch, GROUP_HEADS * GLA_DK, GROUP_W), F32)],
        compiler_params=_cparams(("arbitrary", "arbitrary")),
        name="gla_mixer",
    )(pb.reshape(B, S, PB_W), w2p, gb, ng)
    return out.reshape(T, GROUP_W)


PERM = 256
DIL_UNROLL = 8
COMB_UNROLL = 4


def _perm_matrix(dil, transpose=False):
    i = lax.broadcasted_iota(jnp.int32, (PERM, PERM), 0)
    j = lax.broadcasted_iota(jnp.int32, (PERM, PERM), 1)
    if transpose:
        i, j = j, i
    g = PERM // dil
    return jnp.where(j == (i % g) * dil + i // g, 1.0, 0.0).astype(BF16)


def _dil_kernel(pc_ref, o_ref, p4_ref, p16_ref, ro4_ref, rl4_ref, ro16_ref, rl16_ref):
    S = pc_ref.shape[0]
    BLK = DIL_BLOCK
    nblk = S // BLK
    qi = lax.broadcasted_iota(jnp.int32, (2 * BLK, 2 * BLK), 0) % BLK
    kj = lax.broadcasted_iota(jnp.int32, (2 * BLK, 2 * BLK), 1)
    band = (kj >= qi) & (kj <= qi + BLK)
    lane = lax.broadcasted_iota(jnp.int32, (BLK, LANES), 1)
    head0 = lane < HEAD_DIM
    zero_bf = jnp.zeros((), BF16)
    strided = ((4, p4_ref, ro4_ref, rl4_ref), (16, p16_ref, ro16_ref, rl16_ref))

    def attend(src, blk):
        row0 = pl.multiple_of(blk * BLK, BLK)
        prow0 = pl.multiple_of(jnp.maximum(blk - 1, 0) * BLK, BLK)
        q = src[pl.ds(row0, BLK), 0:LANES]
        qs = jnp.concatenate([jnp.where(head0, q, zero_bf), jnp.where(head0, zero_bf, q)], axis=0)
        kk = jnp.concatenate([src[pl.ds(prow0, BLK), LANES:2 * LANES],
                              src[pl.ds(row0, BLK), LANES:2 * LANES]], axis=0)
        vv = jnp.concatenate([src[pl.ds(prow0, BLK), 2 * LANES:3 * LANES],
                              src[pl.ds(row0, BLK), 2 * LANES:3 * LANES]], axis=0)
        mask = band & (kj >= jnp.where(blk == 0, BLK, 0))
        s = jnp.where(mask, _dot_nt(qs, kk), NEG_INF)
        m = jnp.max(s, axis=-1, keepdims=True)
        p = jnp.exp(s - m)
        den = jnp.sum(p, axis=-1, keepdims=True)
        on = _dot(p.astype(BF16), vv) / den
        lse = m + jnp.log(den)
        return (jnp.where(head0, on[:BLK], on[BLK:]).astype(BF16),
                jnp.where(head0, lse[:BLK], lse[BLK:]))

    perms = [_perm_matrix(d) for d, _, _, _ in strided]

    def perm_body(it, carry):
        for u in range(2):
            blk = it * 2 + u
            xblk = pc_ref[pl.ds(pl.multiple_of(blk * PERM, PERM), PERM), :]
            for (dil, dst, _, _), pm in zip(strided, perms):
                g = PERM // dil
                y = _dot(pm, xblk).astype(BF16)
                g0 = pl.multiple_of(blk * g, g)
                for r in range(dil):
                    dst[r, pl.ds(g0, g), :] = y[r * g:(r + 1) * g]
        return carry

    lax.fori_loop(0, S // PERM // 2, perm_body, 0)

    for dil, src, ro, rl in strided:
        nb = nblk // dil

        def body(it, carry, nb=nb, src=src, ro=ro, rl=rl):
            for u in range(DIL_UNROLL):
                flat = it * DIL_UNROLL + u
                r = flat // nb
                blk = flat % nb
                o, l = attend(src.at[r], blk)
                row0 = pl.multiple_of(blk * BLK, BLK)
                ro[r, pl.ds(row0, BLK), :] = o
                rl[r, pl.ds(row0, BLK), :] = l
            return carry

        lax.fori_loop(0, nblk // DIL_UNROLL, body, 0)

    unperms = [_perm_matrix(d, transpose=True) for d, _, _, _ in strided]

    def comb_body(it, carry):
        for u in range(COMB_UNROLL):
            blk = it * COMB_UNROLL + u
            outs, lses = [], []
            for h in range(PERM // BLK):
                o, l = attend(pc_ref, blk * (PERM // BLK) + h)
                outs.append(o)
                lses.append(l)
            branches = [(jnp.concatenate(outs, axis=0).astype(F32), jnp.concatenate(lses, axis=0))]
            for (dil, _, ro, rl), um in zip(strided, unperms):
                g = PERM // dil
                g0 = pl.multiple_of(blk * g, g)
                go = jnp.concatenate([ro[r, pl.ds(g0, g), :] for r in range(dil)], axis=0)
                gl = jnp.concatenate([rl[r, pl.ds(g0, g), :] for r in range(dil)], axis=0)
                hi = gl.astype(BF16)
                rem = gl - hi.astype(F32)
                mid = rem.astype(BF16)
                lo = (rem - mid.astype(F32)).astype(BF16)
                a = _dot(um, jnp.concatenate([go, hi], axis=1))
                b = _dot(um, jnp.concatenate([mid, lo], axis=1))
                branches.append((a[:, :LANES], (a[:, LANES:] + b[:, :LANES]) + b[:, LANES:]))
            m = jnp.maximum(jnp.maximum(branches[0][1], branches[1][1]), branches[2][1])
            es = [jnp.exp(l - m) for _, l in branches]
            den = es[0] + es[1] + es[2]
            num = es[0] * branches[0][0] + es[1] * branches[1][0] + es[2] * branches[2][0]
            o_ref[pl.ds(pl.multiple_of(blk * PERM, PERM), PERM), :] = (num / den).astype(BF16)
        return carry

    lax.fori_loop(0, S // PERM // COMB_UNROLL, comb_body, 0)


def _dilated_mixer(pc, B, S):
    v1 = pc.reshape(2, B, S, PC_W)
    oc = pl.pallas_call(
        _dil_kernel,
        grid=(B, 2),
        in_specs=[pl.BlockSpec((None, None, S, PC_W), lambda b, p: (p, b, 0, 0))],
        out_specs=pl.BlockSpec((None, S, LANES), lambda b, p: (b, 0, p)),
        out_shape=jax.ShapeDtypeStruct((B, S, GROUP_W), BF16),
        scratch_shapes=[
            pltpu.VMEM((4, S // 4, PC_W), BF16),
            pltpu.VMEM((16, S // 16, PC_W), BF16),
            pltpu.VMEM((4, S // 4, LANES), BF16),
            pltpu.VMEM((4, S // 4, LANES), F32),
            pltpu.VMEM((16, S // 16, LANES), BF16),
            pltpu.VMEM((16, S // 16, LANES), F32),
        ],
        compiler_params=_cparams(("parallel", "parallel"), VMEM_LIMIT),
        name="dilated_mixer",
    )(v1)
    return oc.reshape(B * S, GROUP_W)


FFN_HALO = 8
FFN_CHUNK = 256


def _ffn_kernel(x_ref, oa_ref, ob_ref, oc_ref, od_ref, g1_ref, sh_ref, sc_ref, g2_ref, lng_ref,
                wout_ref, wup_ref, cw_ref, cb_ref, wdn_ref, lnf_ref, o_ref, act_ref, zprev_ref,
                *, nps, final):
    tm = x_ref.shape[0]

    @pl.when((pl.program_id(0) % nps) == 0)
    def _():
        zprev_ref[...] = jnp.zeros_like(zprev_ref)

    cat = jnp.concatenate([oa_ref[...], ob_ref[...], oc_ref[...], od_ref[...]], axis=1)
    x1 = x_ref[...] + g1_ref[...] * _dot(cat, wout_ref[...])
    h = _rms_mod(x1, lng_ref[...], sc_ref[...], sh_ref[...]).astype(BF16)

    def conv(cols):
        z = _dot(h, wup_ref[:, cols])
        ze = jnp.concatenate([zprev_ref[:, cols], z], axis=0)
        zprev_ref[:, cols] = z[tm - FFN_HALO:]
        cw = cw_ref[:, cols]
        return (cw[2:3] * z + cw[1:2] * ze[FFN_HALO - 1:tm + FFN_HALO - 1]
                + cw[0:1] * ze[FFN_HALO - 2:tm + FFN_HALO - 2] + cb_ref[:, cols])

    for c in range(D_FF // FFN_CHUNK):
        a = conv(slice(c * FFN_CHUNK, (c + 1) * FFN_CHUNK))
        b = conv(slice(D_FF + c * FFN_CHUNK, D_FF + (c + 1) * FFN_CHUNK))
        act_ref[:, c * FFN_CHUNK:(c + 1) * FFN_CHUNK] = (_silu(a) * b).astype(BF16)
    y = x1 + g2_ref[...] * _dot(act_ref[...], wdn_ref[...])
    if final:
        ms = jnp.mean(y * y, axis=-1, keepdims=True)
        y = y * lax.rsqrt(ms + EPS) * lnf_ref[...]
    o_ref[...] = y


def _ffn(x2, mod3, oa, ob, oc, od, w_out, ln_g, w_up, conv_w, conv_b, w_down, lnf_g, S, tm, final):
    T, D = x2.shape
    nps = S // tm
    full = lambda *shape: pl.BlockSpec(shape, lambda i: (0,) * len(shape), pipeline_mode=pl.Buffered(1))
    row = lambda w: pl.BlockSpec((tm, w), lambda i: (i, 0))
    modspec = lambda c: pl.BlockSpec((None, 1, D), lambda i: (i // nps, 0, c))
    return pl.pallas_call(
        functools.partial(_ffn_kernel, nps=nps, final=final),
        grid=(T // tm,),
        in_specs=[
            row(D), row(GROUP_W), row(GROUP_W), row(GROUP_W), row(GROUP_W),
            modspec(2), modspec(3), modspec(4), modspec(5),
            full(1, D),
            full(D, D),
            full(D, 2 * D_FF),
            full(FFN_CONV_K, 2 * D_FF),
            full(1, 2 * D_FF),
            full(D_FF, D),
            full(1, D),
        ],
        out_specs=row(D),
        out_shape=jax.ShapeDtypeStruct((T, D), F32),
        scratch_shapes=[pltpu.VMEM((tm, D_FF), BF16), pltpu.VMEM((FFN_HALO, 2 * D_FF), F32)],
        compiler_params=_cparams(("arbitrary",), VMEM_LIMIT),
        name="outproj_ffn",
    )(x2, oa, ob, oc, od, mod3, mod3, mod3, mod3, ln_g, w_out, w_up, conv_w, conv_b, w_down, lnf_g)


def _rearrange_w_in(w_in):
    offs = np.cumsum((0, 256, 256, 128, 128, 256, 256, 16, 256, 256, 256, 256, 256))
    a_val, a_gate, b_q, b_k, b_v, b_g, b_lr, c_q, c_k, c_v, d_u, d_v = [
        w_in[..., offs[i]:offs[i + 1]] for i in range(12)]
    lr_pad = jnp.zeros(w_in.shape[:-1] + (LANES - GLA_RANK,), w_in.dtype)
    cols = [a_val, a_gate, b_q, b_k, b_v, b_g, b_lr, lr_pad]
    for p in range(2):
        s = slice(p * LANES, (p + 1) * LANES)
        cols += [c_q[..., s], c_k[..., s], c_v[..., s]]
    cols += [d_u, d_v]
    return jnp.concatenate(cols, axis=-1).astype(BF16)


def _rope_tables(S):
    inv = 1.0 / (ROPE_THETA ** (jnp.arange(0, HEAD_DIM, 2, dtype=F32) / HEAD_DIM))
    ang = jnp.arange(S, dtype=F32)[:, None] * inv[None, :]
    cos, sin = jnp.cos(ang), jnp.sin(ang)
    cos_t = jnp.concatenate([cos, cos, cos, cos], axis=-1)
    sin_t = jnp.concatenate([-sin, sin, -sin, sin], axis=-1)
    return cos_t, sin_t


def kernel(x, c, ada_w, ada_b, ln1_g, w_in, conv_w, conv_b, cln_g, cln_b, gla_w2, gla_b, gla_norm_g, sgu_ln_g, sgu_ln_b, sgu_w, sgu_b, w_out, ln2_g, ffn_up, ffn_conv_w, ffn_conv_b, ffn_down, lnf_g):
    B, S, D = x.shape
    L = ada_w.shape[0]
    T = B * S
    assert D == D_MODEL and S % (16 * DIL_BLOCK) == 0
    assert all(w // d == DIL_BLOCK for w, d in DIL_PAIRS)
    tm = 1024
    t_ffn = 512
    t_conv = 256
    t_gla = 256

    mod = _modulation(c, ada_w, ada_b)
    wp = _rearrange_w_in(w_in)
    w_out_b = w_out.astype(BF16)
    w_up_b = ffn_up.astype(BF16)
    w_dn_b = ffn_down.astype(BF16)
    w2p = jnp.pad(gla_w2, ((0, 0), (0, LANES - GLA_RANK), (0, 0))).astype(BF16)
    cos_t, sin_t = _rope_tables(S)
    sgu_bias = jnp.repeat(jnp.swapaxes(sgu_b, 1, 2), HEAD_DIM, axis=2)
    ng = jnp.tile(gla_norm_g, (1, GROUP_HEADS))
    row = lambda a, l: a[l][None, :]

    x2 = x.reshape(T, D)
    for l in range(L):
        mod3 = mod[l].reshape(B, 1, 6 * D)
        ha, pb, pc, od = _inproj(x2, mod3, row(ln1_g, l), wp[l], cos_t, sin_t, sgu_w[l], sgu_bias[l],
                                 row(sgu_ln_g, l), row(sgu_ln_b, l), S, tm)
        oa = _conv_mixer(ha, conv_w[l], row(conv_b, l), row(cln_g, l), row(cln_b, l), S, t_conv)
        ob = _gla_mixer(pb, w2p[l], row(gla_b, l), row(ng, l), B, S, t_gla, 4 if B % 4 == 0 else 1)
        oc = _dilated_mixer(pc, B, S)
        x2 = _ffn(x2, mod3, oa, ob, oc, od, w_out_b[l], row(ln2_g, l), w_up_b[l], ffn_conv_w[l],
                  row(ffn_conv_b, l), w_dn_b[l], lnf_g[None, :], S, t_ffn, final=(l == L - 1))
    return x2.reshape(B, S, D)
```

```python
import functools

import numpy as np
import jax
import jax.numpy as jnp
from jax import lax
from jax.experimental import pallas as pl
from jax.experimental.pallas import tpu as pltpu

D_MODEL = 1024
GROUP_W = 256
GROUP_HEADS = 4
HEAD_DIM = 64
CONV_K = 31
GLA_DK = 32
GLA_RANK = 16
GLA_TAU = 16.0
GLA_CHUNK = 64
DIL_PAIRS = ((128, 1), (512, 4), (2048, 16))
DIL_BLOCK = 128
ROPE_THETA = 10000.0
SGU_CHUNK = 128
D_FF = 2816
FFN_CONV_K = 3
EPS = 1e-6
NEG_INF = -1e30
LOG2_E = 1.4426950408889634
LN_2 = 0.6931471805599453
Q_SCALE_LOG2 = HEAD_DIM ** -0.5 * LOG2_E

LANES = 128
BF16 = jnp.bfloat16
F32 = jnp.float32

SEC_A = (0, 512)
SEC_B = (512, 1408)
SEC_C = (1408, 2176)
SEC_D = (2176, 2688)
N_PROJ = 2688
PB_W = SEC_B[1] - SEC_B[0]
PC_W = 384
INPROJ_SUB = 256
CONV_HALO = 32
CONV_SUB = 64

VMEM_LIMIT = 56 * 1024 * 1024


def _cparams(sem, vmem=None):
    return pltpu.CompilerParams(dimension_semantics=sem, vmem_limit_bytes=vmem)


def _dot(a, b):
    return jnp.dot(a, b, preferred_element_type=F32)


def _dot_nt(a, b):
    return lax.dot_general(a, b, (((1,), (1,)), ((), ())), preferred_element_type=F32)


def _dot_tn(a, b):
    return lax.dot_general(a, b, (((0,), (0,)), ((), ())), preferred_element_type=F32)


def _sigmoid(x):
    return 1.0 / (1.0 + jnp.exp(-x))


def _silu(x):
    return x * _sigmoid(x)


def _gelu_tanh(x):
    c = np.sqrt(2.0 / np.pi).astype(np.float32)
    return x * (0.5 * (1.0 + jnp.tanh(c * (x + 0.044715 * (x * x * x)))))


def _rms_mod(x, g, scale, shift):
    ms = jnp.mean(x * x, axis=-1, keepdims=True)
    y = x * lax.rsqrt(ms + EPS) * g
    return y * (1.0 + scale) + shift


def _layernorm(x, g, b):
    mu = jnp.mean(x, axis=-1, keepdims=True)
    xc = x - mu
    var = jnp.mean(xc * xc, axis=-1, keepdims=True)
    return xc * lax.rsqrt(var + EPS) * g + b


def _mod_kernel(c_ref, w_ref, b_ref, o_ref):
    cond = _silu(c_ref[...]).astype(BF16)
    o_ref[...] = _dot(cond, w_ref[...].astype(BF16)) + b_ref[...]


def _modulation(c, ada_w, ada_b):
    L, D, D6 = ada_w.shape
    B = c.shape[0]
    n = D6 // D
    return pl.pallas_call(
        _mod_kernel,
        grid=(L, n),
        in_specs=[
            pl.BlockSpec((B, D), lambda l, j: (0, 0)),
            pl.BlockSpec((None, D, D), lambda l, j: (l, 0, j)),
            pl.BlockSpec((None, 1, D), lambda l, j: (l, 0, j)),
        ],
        out_specs=pl.BlockSpec((None, B, D), lambda l, j: (l, 0, j)),
        out_shape=jax.ShapeDtypeStruct((L, B, D6), F32),
        compiler_params=_cparams(("parallel", "parallel")),
        name="adaln_mod",
    )(c, ada_w, ada_b.reshape(L, 1, D6))


def _inproj_kernel(x_ref, sh_ref, sc_ref, g_ref, w_ref, cos_ref, sin_ref,
                   sw_ref, sbias_ref, slg_ref, slb_ref, cw_ref, cb_ref, clg_ref, clb_ref,
                   oa_ref, pb_ref, pc_ref, od_ref, ext_ref, *, nps):
    tm = x_ref.shape[0]
    ts = INPROJ_SUB

    @pl.when((pl.program_id(0) % nps) == 0)
    def _():
        ext_ref[0, 0:CONV_HALO, :] = jnp.zeros((CONV_HALO, GROUP_W), F32)

    lane = lax.broadcasted_iota(jnp.int32, (ts, LANES), 1)
    first_half = (lane % HEAD_DIM) < (HEAD_DIM // 2)
    row = lax.broadcasted_iota(jnp.int32, (SGU_CHUNK, SGU_CHUNK), 0)
    col = lax.broadcasted_iota(jnp.int32, (SGU_CHUNK, SGU_CHUNK), 1)
    causal = row >= col
    wm = [jnp.where(causal, sw_ref[g], 0.0).astype(BF16) for g in range(GROUP_HEADS)]
    grp = lax.broadcasted_iota(jnp.int32, (SGU_CHUNK, GROUP_W), 1) // HEAD_DIM
    sbias = sbias_ref[...]

    for sub in range(tm // ts):
        rs = slice(sub * ts, (sub + 1) * ts)
        h = _rms_mod(x_ref[rs, :], g_ref[...], sc_ref[...], sh_ref[...]).astype(BF16)

        za = _dot(h, w_ref[:, SEC_A[0]:SEC_A[1]])
        ext_ref[0, CONV_HALO:, :] = za[:, :GROUP_W] * _sigmoid(za[:, GROUP_W:])
        n_shift = ts + CONV_HALO - 8
        for r in range(1, 8):
            ext_ref[r, 0:n_shift, :] = ext_ref[0, r:r + n_shift, :]
        off0 = CONV_HALO - (CONV_K - 1)
        for cs in range(ts // CONV_SUB):
            r0 = cs * CONV_SUB
            acc = jnp.zeros((CONV_SUB, GROUP_W), F32) + cb_ref[...]
            for k in range(CONV_K):
                o = off0 + k
                a0 = r0 + 8 * (o // 8)
                acc = acc + cw_ref[k:k + 1, :] * ext_ref[o % 8, a0:a0 + CONV_SUB, :]
            y = _layernorm(acc, clg_ref[...], clb_ref[...])
            oa_ref[sub * ts + r0:sub * ts + r0 + CONV_SUB, :] = _silu(y).astype(BF16)
        ext_ref[0, 0:CONV_HALO, :] = ext_ref[0, ts:ts + CONV_HALO, :]

        pb_ref[rs, :] = _dot(h, w_ref[:, SEC_B[0]:SEC_B[1]]).astype(BF16)

        zc = _dot(h, w_ref[:, SEC_C[0]:SEC_C[1]])
        cos_t = cos_ref[rs, :]
        sin_t = sin_ref[rs, :]

        def rope(t):
            partner = jnp.where(first_half, pltpu.roll(t, LANES - HEAD_DIM // 2, axis=1),
                                pltpu.roll(t, HEAD_DIM // 2, axis=1))
            return t * cos_t + partner * sin_t

        for p in range(2):
            base = p * PC_W
            q = rope(zc[:, base:base + LANES]) * Q_SCALE_LOG2
            k = rope(zc[:, base + LANES:base + 2 * LANES])
            v = zc[:, base + 2 * LANES:base + 3 * LANES]
            pc_ref[p, rs, 0:LANES] = q.astype(BF16)
            pc_ref[p, rs, LANES:2 * LANES] = k.astype(BF16)
            pc_ref[p, rs, 2 * LANES:3 * LANES] = v.astype(BF16)

        zd = _dot(h, w_ref[:, SEC_D[0]:SEC_D[1]])
        u = _gelu_tanh(zd[:, :GROUP_W])
        vn = _layernorm(_gelu_tanh(zd[:, GROUP_W:]), slg_ref[...], slb_ref[...]).astype(BF16)
        for c in range(ts // SGU_CHUNK):
            rows = slice(c * SGU_CHUNK, (c + 1) * SGU_CHUNK)
            vch = vn[rows]
            sv = sbias
            for g in range(GROUP_HEADS):
                sv = sv + jnp.where(grp == g, _dot(wm[g], vch), 0.0)
            od_ref[sub * ts + c * SGU_CHUNK:sub * ts + (c + 1) * SGU_CHUNK, :] = (u[rows] * sv).astype(BF16)


def _layer_spec(l, *tail, single=False):
    kw = dict(pipeline_mode=pl.Buffered(1)) if single else {}
    return pl.BlockSpec((None,) + tail, lambda *_: (l,) + (0,) * len(tail), **kw)


def _inproj(x2, mod4, l, ln_g, wp, cos_t, sin_t, sgu_w, sgu_bias, sgu_ln_g, sgu_ln_b,
            conv_w, conv_b, cln_g, cln_b, S, tm):
    T, D = x2.shape
    nps = S // tm
    return pl.pallas_call(
        functools.partial(_inproj_kernel, nps=nps),
        grid=(T // tm,),
        in_specs=[
            pl.BlockSpec((tm, D), lambda i: (i, 0)),
            pl.BlockSpec((None, None, 1, D), lambda i: (l, i // nps, 0, 0)),
            pl.BlockSpec((None, None, 1, D), lambda i: (l, i // nps, 0, 1)),
            _layer_spec(l, 1, D),
            _layer_spec(l, D, N_PROJ, single=True),
            pl.BlockSpec((tm, LANES), lambda i: (i % nps, 0)),
            pl.BlockSpec((tm, LANES), lambda i: (i % nps, 0)),
            _layer_spec(l, GROUP_HEADS, SGU_CHUNK, SGU_CHUNK),
            _layer_spec(l, SGU_CHUNK, GROUP_W),
            _layer_spec(l, 1, GROUP_W),
            _layer_spec(l, 1, GROUP_W),
            _layer_spec(l, CONV_K, GROUP_W),
            _layer_spec(l, 1, GROUP_W),
            _layer_spec(l, 1, GROUP_W),
            _layer_spec(l, 1, GROUP_W),
        ],
        out_specs=[
            pl.BlockSpec((tm, GROUP_W), lambda i: (i, 0)),
            pl.BlockSpec((tm, PB_W), lambda i: (i, 0)),
            pl.BlockSpec((2, tm, PC_W), lambda i: (0, i, 0)),
            pl.BlockSpec((tm, GROUP_W), lambda i: (i, 0)),
        ],
        out_shape=[
            jax.ShapeDtypeStruct((T, GROUP_W), BF16),
            jax.ShapeDtypeStruct((T, PB_W), BF16),
            jax.ShapeDtypeStruct((2, T, PC_W), BF16),
            jax.ShapeDtypeStruct((T, GROUP_W), BF16),
        ],
        scratch_shapes=[pltpu.VMEM((8, INPROJ_SUB + CONV_HALO, GROUP_W), F32)],
        compiler_params=_cparams(("arbitrary",), VMEM_LIMIT),
        name="inproj",
    )(x2, mod4, mod4, ln_g, wp, cos_t, sin_t, sgu_w, sgu_bias, sgu_ln_g, sgu_ln_b,
      conv_w, conv_b, cln_g, cln_b)


def _gla_kernel(pb_ref, w2_ref, gb_ref, ng_ref, o_ref, st_ref):
    nbatch, tb = pb_ref.shape[0], pb_ref.shape[1]
    C = GLA_CHUNK
    HK = GROUP_HEADS * GLA_DK

    @pl.when(pl.program_id(1) == 0)
    def _():
        st_ref[...] = jnp.zeros_like(st_ref)

    r_i = lax.broadcasted_iota(jnp.int32, (tb, tb), 0)
    c_i = lax.broadcasted_iota(jnp.int32, (tb, tb), 1)
    ltri = jnp.where((r_i >= c_i) & (r_i // C == c_i // C), 1.0, 0.0).astype(BF16)
    att_r = lax.broadcasted_iota(jnp.int32, (C, GROUP_HEADS * C), 0)
    att_c = lax.broadcasted_iota(jnp.int32, (C, GROUP_HEADS * C), 1) % C
    att_causal = att_r >= att_c
    ks_mask = (lax.broadcasted_iota(jnp.int32, (GROUP_HEADS * C, HK), 0) // C
               == lax.broadcasted_iota(jnp.int32, (GROUP_HEADS * C, HK), 1) // GLA_DK)
    vs_mask = (lax.broadcasted_iota(jnp.int32, (GROUP_HEADS * C, GROUP_W), 0) // C
               == lax.broadcasted_iota(jnp.int32, (GROUP_HEADS * C, GROUP_W), 1) // HEAD_DIM)
    st_mask = (lax.broadcasted_iota(jnp.int32, (HK, GROUP_W), 0) // GLA_DK
               == lax.broadcasted_iota(jnp.int32, (HK, GROUP_W), 1) // HEAD_DIM)
    low_half = lax.broadcasted_iota(jnp.int32, (C, LANES), 1) < HEAD_DIM
    zero_bf = jnp.zeros((), BF16)

    w2 = w2_ref[...]
    gb = gb_ref[...]
    ng = ng_ref[...]

    for bb in range(nbatch):
        q = pb_ref[bb, :, 0:HK].astype(F32)
        k = pb_ref[bb, :, HK:2 * HK].astype(F32)
        lr = pb_ref[bb, :, 2 * HK + 2 * GROUP_W:2 * HK + 2 * GROUP_W + LANES]
        pre = _dot(lr, w2) + gb
        gk = (jnp.minimum(pre, 0.0) - jnp.log(1.0 + jnp.exp(-jnp.abs(pre)))) * (1.0 / GLA_TAU)
        g1 = gk.astype(BF16)
        rem = gk - g1.astype(F32)
        g2 = rem.astype(BF16)
        g3 = (rem - g2.astype(F32)).astype(BF16)
        b = _dot(ltri, g1) + _dot(ltri, g2) + _dot(ltri, g3)
        q_t = (q * (GLA_DK ** -0.5) * jnp.exp(b)).astype(BF16)
        k_t = (k * jnp.exp(-b)).astype(BF16)
        st = st_ref[bb]
        for ch in range(tb // C):
            rows = slice(ch * C, (ch + 1) * C)
            v = pb_ref[bb, rows, 2 * HK:2 * HK + GROUP_W]
            g = pb_ref[bb, rows, 2 * HK + GROUP_W:2 * HK + 2 * GROUP_W].astype(F32)
            b_c = b[rows]
            b_last = b_c[C - 1:C, :]
            k_end = (k[rows] * jnp.exp(b_last - b_c)).astype(BF16)
            q_c = q_t[rows]
            k_stack = jnp.where(ks_mask, jnp.concatenate([k_t[rows]] * GROUP_HEADS, axis=0), zero_bf)
            att = jnp.where(att_causal, _dot_nt(q_c, k_stack), 0.0).astype(BF16)
            v_stack = jnp.where(vs_mask, jnp.concatenate([v] * GROUP_HEADS, axis=0), zero_bf)
            o = _dot(att, v_stack) + _dot(q_c, st.astype(BF16))
            dec_col = jnp.exp(jnp.transpose(b_c[C - 8:C, :])[:, 7:8])
            st = st * dec_col + jnp.where(st_mask, _dot_tn(k_end, v), 0.0)
            invs = []
            for grp in range(GROUP_W // LANES):
                og = o[:, grp * LANES:(grp + 1) * LANES]
                o2 = og * og
                lo = jnp.sum(jnp.where(low_half, o2, 0.0), axis=-1, keepdims=True) * (1.0 / HEAD_DIM)
                hi = jnp.sum(jnp.where(low_half, 0.0, o2), axis=-1, keepdims=True) * (1.0 / HEAD_DIM)
                invs.append(jnp.where(low_half, lax.rsqrt(lo + EPS), lax.rsqrt(hi + EPS)))
            inv = jnp.concatenate(invs, axis=1)
            o_ref[bb, rows, :] = (o * inv * ng * _silu(g)).astype(BF16)
        st_ref[bb] = st


def _gla_mixer(pb, l, w2p, gb, ng, B, S, tb, nbatch):
    T = pb.shape[0]
    nps = S // tb
    out = pl.pallas_call(
        _gla_kernel,
        grid=(B // nbatch, nps),
        in_specs=[
            pl.BlockSpec((nbatch, tb, PB_W), lambda b, i: (b, i, 0)),
            _layer_spec(l, LANES, LANES),
            _layer_spec(l, 1, LANES),
            _layer_spec(l, 1, GROUP_W),
        ],
        out_specs=pl.BlockSpec((nbatch, tb, GROUP_W), lambda b, i: (b, i, 0)),
        out_shape=jax.ShapeDtypeStruct((B, S, GROUP_W), BF16),
        scratch_shapes=[pltpu.VMEM((nbatch, GROUP_HEADS * GLA_DK, GROUP_W), F32)],
        compiler_params=_cparams(("arbitrary", "arbitrary")),
        name="gla_mixer",
    )(pb.reshape(B, S, PB_W), w2p, gb, ng)
    return out.reshape(T, GROUP_W)


PERM = 256
DIL_UNROLL = 16
COMB_UNROLL = 4


def _perm_matrix(dil, transpose=False):
    i = lax.broadcasted_iota(jnp.int32, (PERM, PERM), 0)
    j = lax.broadcasted_iota(jnp.int32, (PERM, PERM), 1)
    if transpose:
        i, j = j, i
    g = PERM // dil
    return jnp.where(j == (i % g) * dil + i // g, 1.0, 0.0).astype(BF16)


def _dil_kernel(pc_ref, o_ref, p4_ref, p16_ref, ro4_ref, rl4_ref, ro16_ref, rl16_ref):
    S = pc_ref.shape[0]
    BLK = DIL_BLOCK
    nblk = S // BLK
    qi = lax.broadcasted_iota(jnp.int32, (2 * BLK, 2 * BLK), 0) % BLK
    kj = lax.broadcasted_iota(jnp.int32, (2 * BLK, 2 * BLK), 1)
    band = (kj >= qi) & (kj <= qi + BLK)
    lane = lax.broadcasted_iota(jnp.int32, (BLK, LANES), 1)
    head0 = lane < HEAD_DIM
    zero_bf = jnp.zeros((), BF16)
    strided = ((4, p4_ref, ro4_ref, rl4_ref), (16, p16_ref, ro16_ref, rl16_ref))

    def attend(src, blk):
        row0 = pl.multiple_of(blk * BLK, BLK)
        prow0 = pl.multiple_of(jnp.maximum(blk - 1, 0) * BLK, BLK)
        q = src[pl.ds(row0, BLK), 0:LANES]
        qs = jnp.concatenate([jnp.where(head0, q, zero_bf), jnp.where(head0, zero_bf, q)], axis=0)
        kk = jnp.concatenate([src[pl.ds(prow0, BLK), LANES:2 * LANES],
                              src[pl.ds(row0, BLK), LANES:2 * LANES]], axis=0)
        vv = jnp.concatenate([src[pl.ds(prow0, BLK), 2 * LANES:3 * LANES],
                              src[pl.ds(row0, BLK), 2 * LANES:3 * LANES]], axis=0)
        mask = band & (kj >= jnp.where(blk == 0, BLK, 0))
        s = jnp.where(mask, _dot_nt(qs, kk), NEG_INF)
        m = jnp.max(s, axis=-1, keepdims=True)
        p = jnp.exp2(s - m)
        den = jnp.sum(p, axis=-1, keepdims=True)
        on = _dot(p.astype(BF16), vv) / den
        lse = (m + jnp.log2(den)) * LN_2
        return (jnp.where(head0, on[:BLK], on[BLK:]).astype(BF16),
                jnp.where(head0, lse[:BLK], lse[BLK:]))

    perms = [_perm_matrix(d) for d, _, _, _ in strided]

    def perm_body(it, carry):
        for u in range(2):
            blk = it * 2 + u
            xblk = pc_ref[pl.ds(pl.multiple_of(blk * PERM, PERM), PERM), :]
            for (dil, dst, _, _), pm in zip(strided, perms):
                g = PERM // dil
                y = _dot(pm, xblk).astype(BF16)
                g0 = pl.multiple_of(blk * g, g)
                for r in range(dil):
                    dst[r, pl.ds(g0, g), :] = y[r * g:(r + 1) * g]
        return carry

    lax.fori_loop(0, S // PERM // 2, perm_body, 0)

    for dil, src, ro, rl in strided:
        nb = nblk // dil

        def body(it, carry, nb=nb, src=src, ro=ro, rl=rl):
            for u in range(DIL_UNROLL):
                flat = it * DIL_UNROLL + u
                r = flat // nb
                blk = flat % nb
                o, l = attend(src.at[r], blk)
                row0 = pl.multiple_of(blk * BLK, BLK)
                ro[r, pl.ds(row0, BLK), :] = o
                rl[r, pl.ds(row0, BLK), :] = l
            return carry

        lax.fori_loop(0, nblk // DIL_UNROLL, body, 0)

    unperms = [_perm_matrix(d, transpose=True) for d, _, _, _ in strided]

    def comb_body(it, carry):
        for u in range(COMB_UNROLL):
            blk = it * COMB_UNROLL + u
            outs, lses = [], []
            for h in range(PERM // BLK):
                o, l = attend(pc_ref, blk * (PERM // BLK) + h)
                outs.append(o)
                lses.append(l)
            branches = [(jnp.concatenate(outs, axis=0).astype(F32), jnp.concatenate(lses, axis=0))]
            for (dil, _, ro, rl), um in zip(strided, unperms):
                g = PERM // dil
                g0 = pl.multiple_of(blk * g, g)
                go = jnp.concatenate([ro[r, pl.ds(g0, g), :] for r in range(dil)], axis=0)
                gl = jnp.concatenate([rl[r, pl.ds(g0, g), :] for r in range(dil)], axis=0)
                hi = gl.astype(BF16)
                rem = gl - hi.astype(F32)
                mid = rem.astype(BF16)
                lo = (rem - mid.astype(F32)).astype(BF16)
                a = _dot(um, jnp.concatenate([go, hi], axis=1))
                b = _dot(um, jnp.concatenate([mid, lo], axis=1))
                branches.append((a[:, :LANES], (a[:, LANES:] + b[:, :LANES]) + b[:, LANES:]))
            m = jnp.maximum(jnp.maximum(branches[0][1], branches[1][1]), branches[2][1])
            es = [jnp.exp(l - m) for _, l in branches]
            den = es[0] + es[1] + es[2]
            num = es[0] * branches[0][0] + es[1] * branches[1][0] + es[2] * branches[2][0]
            o_ref[pl.ds(pl.multiple_of(blk * PERM, PERM), PERM), :] = (num / den).astype(BF16)
        return carry

    lax.fori_loop(0, S // PERM // COMB_UNROLL, comb_body, 0)


def _dilated_mixer(pc, B, S):
    v1 = pc.reshape(2, B, S, PC_W)
    oc = pl.pallas_call(
        _dil_kernel,
        grid=(B, 2),
        in_specs=[pl.BlockSpec((None, None, S, PC_W), lambda b, p: (p, b, 0, 0))],
        out_specs=pl.BlockSpec((None, S, LANES), lambda b, p: (b, 0, p)),
        out_shape=jax.ShapeDtypeStruct((B, S, GROUP_W), BF16),
        scratch_shapes=[
            pltpu.VMEM((4, S // 4, PC_W), BF16),
            pltpu.VMEM((16, S // 16, PC_W), BF16),
            pltpu.VMEM((4, S // 4, LANES), BF16),
            pltpu.VMEM((4, S // 4, LANES), F32),
            pltpu.VMEM((16, S // 16, LANES), BF16),
            pltpu.VMEM((16, S // 16, LANES), F32),
        ],
        compiler_params=_cparams(("parallel", "parallel"), VMEM_LIMIT),
        name="dilated_mixer",
    )(v1)
    return oc.reshape(B * S, GROUP_W)


FFN_HALO = 8
FFN_CHUNK = 256
FFN_ROWS = 256


def _ffn_kernel(x_ref, oa_ref, ob_ref, oc_ref, od_ref, g1_ref, sh_ref, sc_ref, g2_ref, lng_ref,
                wout_ref, wup_ref, cw_ref, cb_ref, wdn_ref, lnf_ref, o_ref, act_ref, zprev_ref,
                *, nps, final):
    tm = x_ref.shape[0]

    @pl.when((pl.program_id(0) % nps) == 0)
    def _():
        zprev_ref[...] = jnp.zeros_like(zprev_ref)

    cat = jnp.concatenate([oa_ref[...], ob_ref[...], oc_ref[...], od_ref[...]], axis=1)
    x1 = x_ref[...] + g1_ref[...] * _dot(cat, wout_ref[...])
    h = _rms_mod(x1, lng_ref[...], sc_ref[...], sh_ref[...]).astype(BF16)

    def conv(z, zp, cols):
        ze = jnp.concatenate([zp, z], axis=0)
        cw = cw_ref[:, cols]
        rb = z.shape[0]
        return (cw[2:3] * z + cw[1:2] * ze[FFN_HALO - 1:rb + FFN_HALO - 1]
                + cw[0:1] * ze[FFN_HALO - 2:rb + FFN_HALO - 2] + cb_ref[:, cols])

    for c in range(D_FF // FFN_CHUNK):
        ca = slice(c * FFN_CHUNK, (c + 1) * FFN_CHUNK)
        cb = slice(D_FF + c * FFN_CHUNK, D_FF + (c + 1) * FFN_CHUNK)
        pa = zprev_ref[:, ca]
        pb = zprev_ref[:, cb]
        for r in range(tm // FFN_ROWS):
            rows = slice(r * FFN_ROWS, (r + 1) * FFN_ROWS)
            za = _dot(h[rows], wup_ref[:, ca])
            zb = _dot(h[rows], wup_ref[:, cb])
            act_ref[rows, ca] = (_silu(conv(za, pa, ca)) * conv(zb, pb, cb)).astype(BF16)
            pa = za[FFN_ROWS - FFN_HALO:]
            pb = zb[FFN_ROWS - FFN_HALO:]
        zprev_ref[:, ca] = pa
        zprev_ref[:, cb] = pb
    y = x1 + g2_ref[...] * _dot(act_ref[...], wdn_ref[...])
    if final:
        ms = jnp.mean(y * y, axis=-1, keepdims=True)
        y = y * lax.rsqrt(ms + EPS) * lnf_ref[...]
    o_ref[...] = y


def _ffn(x2, mod4, l, oa, ob, oc, od, w_out, ln_g, w_up, conv_w, conv_b, w_down, lnf_g, S, tm, final):
    T, D = x2.shape
    nps = S // tm
    row = lambda w: pl.BlockSpec((tm, w), lambda i: (i, 0))
    modspec = lambda c: pl.BlockSpec((None, None, 1, D), lambda i: (l, i // nps, 0, c))
    single = functools.partial(_layer_spec, l, single=True)
    return pl.pallas_call(
        functools.partial(_ffn_kernel, nps=nps, final=final),
        grid=(T // tm,),
        in_specs=[
            row(D), row(GROUP_W), row(GROUP_W), row(GROUP_W), row(GROUP_W),
            modspec(2), modspec(3), modspec(4), modspec(5),
            single(1, D),
            single(D, D),
            single(D, 2 * D_FF),
            single(FFN_CONV_K, 2 * D_FF),
            single(1, 2 * D_FF),
            single(D_FF, D),
            pl.BlockSpec((1, D), lambda i: (0, 0), pipeline_mode=pl.Buffered(1)),
        ],
        out_specs=row(D),
        out_shape=jax.ShapeDtypeStruct((T, D), F32),
        scratch_shapes=[pltpu.VMEM((tm, D_FF), BF16), pltpu.VMEM((FFN_HALO, 2 * D_FF), F32)],
        compiler_params=_cparams(("arbitrary",), VMEM_LIMIT),
        name="outproj_ffn",
    )(x2, oa, ob, oc, od, mod4, mod4, mod4, mod4, ln_g, w_out, w_up, conv_w, conv_b, w_down, lnf_g)


def _rearrange_w_in(w_in):
    offs = np.cumsum((0, 256, 256, 128, 128, 256, 256, 16, 256, 256, 256, 256, 256))
    a_val, a_gate, b_q, b_k, b_v, b_g, b_lr, c_q, c_k, c_v, d_u, d_v = [
        w_in[..., offs[i]:offs[i + 1]] for i in range(12)]
    lr_pad = jnp.zeros(w_in.shape[:-1] + (LANES - GLA_RANK,), w_in.dtype)
    cols = [a_val, a_gate, b_q, b_k, b_v, b_g, b_lr, lr_pad]
    for p in range(2):
        s = slice(p * LANES, (p + 1) * LANES)
        cols += [c_q[..., s], c_k[..., s], c_v[..., s]]
    cols += [d_u, d_v]
    return jnp.concatenate(cols, axis=-1).astype(BF16)


def _rope_tables(S):
    inv = 1.0 / (ROPE_THETA ** (jnp.arange(0, HEAD_DIM, 2, dtype=F32) / HEAD_DIM))
    ang = jnp.arange(S, dtype=F32)[:, None] * inv[None, :]
    cos, sin = jnp.cos(ang), jnp.sin(ang)
    cos_t = jnp.concatenate([cos, cos, cos, cos], axis=-1)
    sin_t = jnp.concatenate([-sin, sin, -sin, sin], axis=-1)
    return cos_t, sin_t


def kernel(x, c, ada_w, ada_b, ln1_g, w_in, conv_w, conv_b, cln_g, cln_b, gla_w2, gla_b, gla_norm_g, sgu_ln_g, sgu_ln_b, sgu_w, sgu_b, w_out, ln2_g, ffn_up, ffn_conv_w, ffn_conv_b, ffn_down, lnf_g):
    B, S, D = x.shape
    L = ada_w.shape[0]
    T = B * S
    assert D == D_MODEL and S % (16 * DIL_BLOCK) == 0
    assert all(w // d == DIL_BLOCK for w, d in DIL_PAIRS)
    tm = 1024
    t_ffn = 512
    t_gla = 256
    nbatch = 4 if B % 4 == 0 else 1

    mod4 = _modulation(c, ada_w, ada_b).reshape(L, B, 1, 6 * D)
    wp = _rearrange_w_in(w_in)
    w_out_b = w_out.astype(BF16)
    w_up_b = ffn_up.astype(BF16)
    w_dn_b = ffn_down.astype(BF16)
    w2p = jnp.pad(gla_w2, ((0, 0), (0, LANES - GLA_RANK), (0, 0))).astype(BF16)
    cos_t, sin_t = _rope_tables(S)
    sgu_bias = jnp.repeat(jnp.swapaxes(sgu_b, 1, 2), HEAD_DIM, axis=2)
    ng = jnp.tile(gla_norm_g, (1, GROUP_HEADS))
    vec = lambda a: a[:, None, :]

    x2 = x.reshape(T, D)
    for l in range(L):
        oa, pb, pc, od = _inproj(x2, mod4, l, vec(ln1_g), wp, cos_t, sin_t, sgu_w, sgu_bias,
                                 vec(sgu_ln_g), vec(sgu_ln_b), conv_w, vec(conv_b), vec(cln_g),
                                 vec(cln_b), S, tm)
        ob = _gla_mixer(pb, l, w2p, vec(gla_b), vec(ng), B, S, t_gla, nbatch)
        oc = _dilated_mixer(pc, B, S)
        x2 = _ffn(x2, mod4, l, oa, ob, oc, od, w_out_b, vec(ln2_g), w_up_b, ffn_conv_w,
                  vec(ffn_conv_b), w_dn_b, lnf_g[None, :], S, t_ffn, final=(l == L - 1))
    return x2.reshape(B, S, D)
```

```python
import functools

import numpy as np
import jax
import jax.numpy as jnp
from jax import lax
from jax.experimental import pallas as pl
from jax.experimental.pallas import tpu as pltpu

D_MODEL = 1024
GROUP_W = 256
GROUP_HEADS = 4
HEAD_DIM = 64
CONV_K = 31
GLA_DK = 32
GLA_RANK = 16
GLA_TAU = 16.0
GLA_CHUNK = 64
DIL_PAIRS = ((128, 1), (512, 4), (2048, 16))
DIL_BLOCK = 128
ROPE_THETA = 10000.0
SGU_CHUNK = 128
D_FF = 2816
FFN_CONV_K = 3
EPS = 1e-6
NEG_INF = -1e30
LOG2_E = 1.4426950408889634
Q_SCALE_LOG2 = HEAD_DIM ** -0.5 * LOG2_E

LANES = 128
BF16 = jnp.bfloat16
F32 = jnp.float32

SEC_A = (0, 512)
SEC_B = (512, 1408)
SEC_C = (1408, 2176)
SEC_D = (2176, 2688)
N_PROJ = 2688
PB_W = SEC_B[1] - SEC_B[0]
PC_W = 384
INPROJ_SUB = 256
CONV_HALO = 32
CONV_SUB = 64

VMEM_LIMIT = 56 * 1024 * 1024


def _cparams(sem, vmem=None):
    return pltpu.CompilerParams(dimension_semantics=sem, vmem_limit_bytes=vmem)


def _dot(a, b):
    return jnp.dot(a, b, preferred_element_type=F32)


def _dot_nt(a, b):
    return lax.dot_general(a, b, (((1,), (1,)), ((), ())), preferred_element_type=F32)


def _dot_tn(a, b):
    return lax.dot_general(a, b, (((0,), (0,)), ((), ())), preferred_element_type=F32)


def _sigmoid(x):
    return 1.0 / (1.0 + jnp.exp(-x))


def _silu(x):
    return x * _sigmoid(x)


def _gelu_tanh(x):
    c = np.sqrt(2.0 / np.pi).astype(np.float32)
    return x * (0.5 * (1.0 + jnp.tanh(c * (x + 0.044715 * (x * x * x)))))


def _rms_mod(x, g, scale, shift):
    ms = jnp.mean(x * x, axis=-1, keepdims=True)
    y = x * lax.rsqrt(ms + EPS) * g
    return y * (1.0 + scale) + shift


def _layernorm(x, g, b):
    mu = jnp.mean(x, axis=-1, keepdims=True)
    xc = x - mu
    var = jnp.mean(xc * xc, axis=-1, keepdims=True)
    return xc * lax.rsqrt(var + EPS) * g + b


def _mod_kernel(c_ref, w_ref, b_ref, o_ref):
    cond = _silu(c_ref[...]).astype(BF16)
    o_ref[...] = _dot(cond, w_ref[...].astype(BF16)) + b_ref[...]


def _modulation(c, ada_w, ada_b):
    L, D, D6 = ada_w.shape
    B = c.shape[0]
    n = D6 // D
    return pl.pallas_call(
        _mod_kernel,
        grid=(L, n),
        in_specs=[
            pl.BlockSpec((B, D), lambda l, j: (0, 0)),
            pl.BlockSpec((None, D, D), lambda l, j: (l, 0, j)),
            pl.BlockSpec((None, 1, D), lambda l, j: (l, 0, j)),
        ],
        out_specs=pl.BlockSpec((None, B, D), lambda l, j: (l, 0, j)),
        out_shape=jax.ShapeDtypeStruct((L, B, D6), F32),
        compiler_params=_cparams(("parallel", "parallel")),
        name="adaln_mod",
    )(c, ada_w, ada_b.reshape(L, 1, D6))


def _inproj_kernel(x_ref, sh_ref, sc_ref, g_ref, w_ref, cos_ref, sin_ref,
                   sw_ref, sbias_ref, slg_ref, slb_ref, cw_ref, cb_ref, clg_ref, clb_ref,
                   oa_ref, pb_ref, pc_ref, od_ref, ext_ref, *, nps):
    tm = x_ref.shape[0]
    ts = INPROJ_SUB

    @pl.when((pl.program_id(0) % nps) == 0)
    def _():
        ext_ref[0, 0:CONV_HALO, :] = jnp.zeros((CONV_HALO, GROUP_W), F32)

    lane = lax.broadcasted_iota(jnp.int32, (ts, LANES), 1)
    first_half = (lane % HEAD_DIM) < (HEAD_DIM // 2)
    row = lax.broadcasted_iota(jnp.int32, (SGU_CHUNK, SGU_CHUNK), 0)
    col = lax.broadcasted_iota(jnp.int32, (SGU_CHUNK, SGU_CHUNK), 1)
    causal = row >= col
    wm = [jnp.where(causal, sw_ref[g], 0.0).astype(BF16) for g in range(GROUP_HEADS)]
    grp = lax.broadcasted_iota(jnp.int32, (SGU_CHUNK, GROUP_W), 1) // HEAD_DIM
    sbias = sbias_ref[...]

    for sub in range(tm // ts):
        rs = slice(sub * ts, (sub + 1) * ts)
        h = _rms_mod(x_ref[rs, :], g_ref[...], sc_ref[...], sh_ref[...]).astype(BF16)

        za = _dot(h, w_ref[:, SEC_A[0]:SEC_A[1]])
        ext_ref[0, CONV_HALO:, :] = za[:, :GROUP_W] * _sigmoid(za[:, GROUP_W:])
        n_shift = ts + CONV_HALO - 8
        for r in range(1, 8):
            ext_ref[r, 0:n_shift, :] = ext_ref[0, r:r + n_shift, :]
        off0 = CONV_HALO - (CONV_K - 1)
        for cs in range(ts // CONV_SUB):
            r0 = cs * CONV_SUB
            acc = jnp.zeros((CONV_SUB, GROUP_W), F32) + cb_ref[...]
            for k in range(CONV_K):
                o = off0 + k
                a0 = r0 + 8 * (o // 8)
                acc = acc + cw_ref[k:k + 1, :] * ext_ref[o % 8, a0:a0 + CONV_SUB, :]
            y = _layernorm(acc, clg_ref[...], clb_ref[...])
            oa_ref[sub * ts + r0:sub * ts + r0 + CONV_SUB, :] = _silu(y).astype(BF16)
        ext_ref[0, 0:CONV_HALO, :] = ext_ref[0, ts:ts + CONV_HALO, :]

        pb_ref[rs, :] = _dot(h, w_ref[:, SEC_B[0]:SEC_B[1]]).astype(BF16)

        zc = _dot(h, w_ref[:, SEC_C[0]:SEC_C[1]])
        cos_t = cos_ref[rs, :]
        sin_t = sin_ref[rs, :]

        def rope(t):
            partner = jnp.where(first_half, pltpu.roll(t, LANES - HEAD_DIM // 2, axis=1),
                                pltpu.roll(t, HEAD_DIM // 2, axis=1))
            return t * cos_t + partner * sin_t

        for p in range(2):
            q = rope(zc[:, p * LANES:(p + 1) * LANES]) * Q_SCALE_LOG2
            k = rope(zc[:, GROUP_W + p * LANES:GROUP_W + (p + 1) * LANES])
            v = zc[:, 2 * GROUP_W + p * LANES:2 * GROUP_W + (p + 1) * LANES]
            pc_ref[p, rs, 0:LANES] = q.astype(BF16)
            pc_ref[p, rs, LANES:2 * LANES] = k.astype(BF16)
            pc_ref[p, rs, 2 * LANES:3 * LANES] = v.astype(BF16)

        zd = _dot(h, w_ref[:, SEC_D[0]:SEC_D[1]])
        u = _gelu_tanh(zd[:, :GROUP_W])
        vn = _layernorm(_gelu_tanh(zd[:, GROUP_W:]), slg_ref[...], slb_ref[...]).astype(BF16)
        for c in range(ts // SGU_CHUNK):
            rows = slice(c * SGU_CHUNK, (c + 1) * SGU_CHUNK)
            vch = vn[rows]
            sv = sbias
            for g in range(GROUP_HEADS):
                sv = sv + jnp.where(grp == g, _dot(wm[g], vch), 0.0)
            od_ref[sub * ts + c * SGU_CHUNK:sub * ts + (c + 1) * SGU_CHUNK, :] = (u[rows] * sv).astype(BF16)


def _layer_spec(l, *tail, single=False):
    kw = dict(pipeline_mode=pl.Buffered(1)) if single else {}
    return pl.BlockSpec((None,) + tail, lambda *_: (l,) + (0,) * len(tail), **kw)


def _inproj(x2, mod4, l, ln_g, wp, cos_t, sin_t, sgu_w, sgu_bias, sgu_ln_g, sgu_ln_b,
            conv_w, conv_b, cln_g, cln_b, S, tm):
    T, D = x2.shape
    nps = S // tm
    return pl.pallas_call(
        functools.partial(_inproj_kernel, nps=nps),
        grid=(T // tm,),
        in_specs=[
            pl.BlockSpec((tm, D), lambda i: (i, 0)),
            pl.BlockSpec((None, None, 1, D), lambda i: (l, i // nps, 0, 0)),
            pl.BlockSpec((None, None, 1, D), lambda i: (l, i // nps, 0, 1)),
            _layer_spec(l, 1, D),
            _layer_spec(l, D, N_PROJ, single=True),
            pl.BlockSpec((tm, LANES), lambda i: (i % nps, 0)),
            pl.BlockSpec((tm, LANES), lambda i: (i % nps, 0)),
            _layer_spec(l, GROUP_HEADS, SGU_CHUNK, SGU_CHUNK),
            _layer_spec(l, SGU_CHUNK, GROUP_W),
            _layer_spec(l, 1, GROUP_W),
            _layer_spec(l, 1, GROUP_W),
            _layer_spec(l, CONV_K, GROUP_W),
            _layer_spec(l, 1, GROUP_W),
            _layer_spec(l, 1, GROUP_W),
            _layer_spec(l, 1, GROUP_W),
        ],
        out_specs=[
            pl.BlockSpec((tm, GROUP_W), lambda i: (i, 0)),
            pl.BlockSpec((tm, PB_W), lambda i: (i, 0)),
            pl.BlockSpec((2, tm, PC_W), lambda i: (0, i, 0)),
            pl.BlockSpec((tm, GROUP_W), lambda i: (i, 0)),
        ],
        out_shape=[
            jax.ShapeDtypeStruct((T, GROUP_W), BF16),
            jax.ShapeDtypeStruct((T, PB_W), BF16),
            jax.ShapeDtypeStruct((2, T, PC_W), BF16),
            jax.ShapeDtypeStruct((T, GROUP_W), BF16),
        ],
        scratch_shapes=[pltpu.VMEM((8, INPROJ_SUB + CONV_HALO, GROUP_W), F32)],
        compiler_params=_cparams(("arbitrary",), VMEM_LIMIT),
        name="inproj",
    )(x2, mod4, mod4, ln_g, wp, cos_t, sin_t, sgu_w, sgu_bias, sgu_ln_g, sgu_ln_b,
      conv_w, conv_b, cln_g, cln_b)


def _gla_kernel(pb_ref, w2_ref, gb_ref, ng_ref, o_ref, st_ref):
    nbatch, tb = pb_ref.shape[0], pb_ref.shape[1]
    C = GLA_CHUNK
    HK = GROUP_HEADS * GLA_DK

    @pl.when(pl.program_id(1) == 0)
    def _():
        st_ref[...] = jnp.zeros_like(st_ref)

    r_i = lax.broadcasted_iota(jnp.int32, (tb, tb), 0)
    c_i = lax.broadcasted_iota(jnp.int32, (tb, tb), 1)
    ltri = jnp.where((r_i >= c_i) & (r_i // C == c_i // C), 1.0, 0.0).astype(BF16)
    att_r = lax.broadcasted_iota(jnp.int32, (C, GROUP_HEADS * C), 0)
    att_c = lax.broadcasted_iota(jnp.int32, (C, GROUP_HEADS * C), 1) % C
    att_causal = att_r >= att_c
    ks_mask = (lax.broadcasted_iota(jnp.int32, (GROUP_HEADS * C, HK), 0) // C
               == lax.broadcasted_iota(jnp.int32, (GROUP_HEADS * C, HK), 1) // GLA_DK)
    vs_mask = (lax.broadcasted_iota(jnp.int32, (GROUP_HEADS * C, GROUP_W), 0) // C
               == lax.broadcasted_iota(jnp.int32, (GROUP_HEADS * C, GROUP_W), 1) // HEAD_DIM)
    st_mask = (lax.broadcasted_iota(jnp.int32, (HK, GROUP_W), 0) // GLA_DK
               == lax.broadcasted_iota(jnp.int32, (HK, GROUP_W), 1) // HEAD_DIM)
    low_half = lax.broadcasted_iota(jnp.int32, (C, LANES), 1) < HEAD_DIM
    zero_bf = jnp.zeros((), BF16)

    w2 = w2_ref[...]
    gb = gb_ref[...]
    ng = ng_ref[...]

    for bb in range(nbatch):
        q = pb_ref[bb, :, 0:HK].astype(F32)
        k = pb_ref[bb, :, HK:2 * HK].astype(F32)
        lr = pb_ref[bb, :, 2 * HK + 2 * GROUP_W:2 * HK + 2 * GROUP_W + LANES]
        pre = _dot(lr, w2) + gb
        gk = (jnp.minimum(pre, 0.0) - jnp.log(1.0 + jnp.exp(-jnp.abs(pre)))) * (1.0 / GLA_TAU)
        g1 = gk.astype(BF16)
        rem = gk - g1.astype(F32)
        g2 = rem.astype(BF16)
        g3 = (rem - g2.astype(F32)).astype(BF16)
        b = _dot(ltri, g1) + _dot(ltri, g2) + _dot(ltri, g3)
        q_t = (q * (GLA_DK ** -0.5) * jnp.exp(b)).astype(BF16)
        k_t = (k * jnp.exp(-b)).astype(BF16)
        st = st_ref[bb]
        for ch in range(tb // C):
            rows = slice(ch * C, (ch + 1) * C)
            v = pb_ref[bb, rows, 2 * HK:2 * HK + GROUP_W]
            g = pb_ref[bb, rows, 2 * HK + GROUP_W:2 * HK + 2 * GROUP_W].astype(F32)
            b_c = b[rows]
            b_last = b_c[C - 1:C, :]
            k_end = (k[rows] * jnp.exp(b_last - b_c)).astype(BF16)
            q_c = q_t[rows]
            k_stack = jnp.where(ks_mask, jnp.concatenate([k_t[rows]] * GROUP_HEADS, axis=0), zero_bf)
            att = jnp.where(att_causal, _dot_nt(q_c, k_stack), 0.0).astype(BF16)
            v_stack = jnp.where(vs_mask, jnp.concatenate([v] * GROUP_HEADS, axis=0), zero_bf)
            o = _dot(att, v_stack) + _dot(q_c, st.astype(BF16))
            dec_col = jnp.exp(jnp.transpose(b_c[C - 8:C, :])[:, 7:8])
            st = st * dec_col + jnp.where(st_mask, _dot_tn(k_end, v), 0.0)
            invs = []
            for grp in range(GROUP_W // LANES):
                og = o[:, grp * LANES:(grp + 1) * LANES]
                o2 = og * og
                lo = jnp.sum(jnp.where(low_half, o2, 0.0), axis=-1, keepdims=True) * (1.0 / HEAD_DIM)
                hi = jnp.sum(jnp.where(low_half, 0.0, o2), axis=-1, keepdims=True) * (1.0 / HEAD_DIM)
                invs.append(jnp.where(low_half, lax.rsqrt(lo + EPS), lax.rsqrt(hi + EPS)))
            inv = jnp.concatenate(invs, axis=1)
            o_ref[bb, rows, :] = (o * inv * ng * _silu(g)).astype(BF16)
        st_ref[bb] = st


def _gla_mixer(pb, l, w2p, gb, ng, B, S, tb, nbatch):
    T = pb.shape[0]
    nps = S // tb
    out = pl.pallas_call(
        _gla_kernel,
        grid=(B // nbatch, nps),
        in_specs=[
            pl.BlockSpec((nbatch, tb, PB_W), lambda b, i: (b, i, 0)),
            _layer_spec(l, LANES, LANES),
            _layer_spec(l, 1, LANES),
            _layer_spec(l, 1, GROUP_W),
        ],
        out_specs=pl.BlockSpec((nbatch, tb, GROUP_W), lambda b, i: (b, i, 0)),
        out_shape=jax.ShapeDtypeStruct((B, S, GROUP_W), BF16),
        scratch_shapes=[pltpu.VMEM((nbatch, GROUP_HEADS * GLA_DK, GROUP_W), F32)],
        compiler_params=_cparams(("arbitrary", "arbitrary")),
        name="gla_mixer",
    )(pb.reshape(B, S, PB_W), w2p, gb, ng)
    return out.reshape(T, GROUP_W)


PERM = 256
DIL_UNROLL = 16
COMB_UNROLL = 4


def _perm_matrix(dil, transpose=False):
    i = lax.broadcasted_iota(jnp.int32, (PERM, PERM), 0)
    j = lax.broadcasted_iota(jnp.int32, (PERM, PERM), 1)
    if transpose:
        i, j = j, i
    g = PERM // dil
    return jnp.where(j == (i % g) * dil + i // g, 1.0, 0.0).astype(BF16)


def _dil_kernel(pc_ref, o_ref, p4_ref, p16_ref, ro4_ref, rl4_ref, ro16_ref, rl16_ref):
    S = pc_ref.shape[0]
    BLK = DIL_BLOCK
    nblk = S // BLK
    qi = lax.broadcasted_iota(jnp.int32, (2 * BLK, 2 * BLK), 0) % BLK
    kj = lax.broadcasted_iota(jnp.int32, (2 * BLK, 2 * BLK), 1)
    band = (kj >= qi) & (kj <= qi + BLK)
    lane = lax.broadcasted_iota(jnp.int32, (BLK, LANES), 1)
    head0 = lane < HEAD_DIM
    zero_bf = jnp.zeros((), BF16)
    strided = ((4, p4_ref, ro4_ref, rl4_ref), (16, p16_ref, ro16_ref, rl16_ref))

    def attend(src, blk):
        row0 = pl.multiple_of(blk * BLK, BLK)
        prow0 = pl.multiple_of(jnp.maximum(blk - 1, 0) * BLK, BLK)
        q = src[pl.ds(row0, BLK), 0:LANES]
        qs = jnp.concatenate([jnp.where(head0, q, zero_bf), jnp.where(head0, zero_bf, q)], axis=0)
        kk = jnp.concatenate([src[pl.ds(prow0, BLK), LANES:2 * LANES],
                              src[pl.ds(row0, BLK), LANES:2 * LANES]], axis=0)
        vv = jnp.concatenate([src[pl.ds(prow0, BLK), 2 * LANES:3 * LANES],
                              src[pl.ds(row0, BLK), 2 * LANES:3 * LANES]], axis=0)
        mask = band & (kj >= jnp.where(blk == 0, BLK, 0))
        s = jnp.where(mask, _dot_nt(qs, kk), NEG_INF)
        m = jnp.max(s, axis=-1, keepdims=True)
        p = jnp.exp2(s - m)
        den = jnp.sum(p, axis=-1, keepdims=True)
        on = _dot(p.astype(BF16), vv) / den
        lse = m + jnp.log2(den)
        return (jnp.where(head0, on[:BLK], on[BLK:]).astype(BF16),
                jnp.where(head0, lse[:BLK], lse[BLK:]))

    perms = [_perm_matrix(d) for d, _, _, _ in strided]

    def perm_body(it, carry):
        for u in range(2):
            blk = it * 2 + u
            xblk = pc_ref[pl.ds(pl.multiple_of(blk * PERM, PERM), PERM), :]
            for (dil, dst, _, _), pm in zip(strided, perms):
                g = PERM // dil
                y = _dot(pm, xblk).astype(BF16)
                g0 = pl.multiple_of(blk * g, g)
                for r in range(dil):
                    dst[r, pl.ds(g0, g), :] = y[r * g:(r + 1) * g]
        return carry

    lax.fori_loop(0, S // PERM // 2, perm_body, 0)

    for dil, src, ro, rl in strided:
        nb = nblk // dil

        def body(it, carry, nb=nb, src=src, ro=ro, rl=rl):
            for u in range(DIL_UNROLL):
                flat = it * DIL_UNROLL + u
                r = flat // nb
                blk = flat % nb
                o, l = attend(src.at[r], blk)
                row0 = pl.multiple_of(blk * BLK, BLK)
                ro[r, pl.ds(row0, BLK), :] = o
                rl[r, pl.ds(row0, BLK), :] = l
            return carry

        lax.fori_loop(0, nblk // DIL_UNROLL, body, 0)

    unperms = [_perm_matrix(d, transpose=True) for d, _, _, _ in strided]

    def comb_body(it, carry):
        for u in range(COMB_UNROLL):
            blk = it * COMB_UNROLL + u
            outs, lses = [], []
            for h in range(PERM // BLK):
                o, l = attend(pc_ref, blk * (PERM // BLK) + h)
                outs.append(o)
                lses.append(l)
            branches = [(jnp.concatenate(outs, axis=0).astype(F32), jnp.concatenate(lses, axis=0))]
            for (dil, _, ro, rl), um in zip(strided, unperms):
                g = PERM // dil
                g0 = pl.multiple_of(blk * g, g)
                go = jnp.concatenate([ro[r, pl.ds(g0, g), :] for r in range(dil)], axis=0)
                gl = jnp.concatenate([rl[r, pl.ds(g0, g), :] for r in range(dil)], axis=0)
                hi = gl.astype(BF16)
                rem = gl - hi.astype(F32)
                mid = rem.astype(BF16)
                lo = (rem - mid.astype(F32)).astype(BF16)
                a = _dot(um, jnp.concatenate([go, hi], axis=1))
                b = _dot(um, jnp.concatenate([mid, lo], axis=1))
                branches.append((a[:, :LANES], (a[:, LANES:] + b[:, :LANES]) + b[:, LANES:]))
            m = jnp.maximum(jnp.maximum(branches[0][1], branches[1][1]), branches[2][1])
            es = [jnp.exp2(l - m) for _, l in branches]
            den = es[0] + es[1] + es[2]
            num = es[0] * branches[0][0] + es[1] * branches[1][0] + es[2] * branches[2][0]
            o_ref[pl.ds(pl.multiple_of(blk * PERM, PERM), PERM), :] = (num / den).astype(BF16)
        return carry

    lax.fori_loop(0, S // PERM // COMB_UNROLL, comb_body, 0)


def _dilated_mixer(pc, B, S):
    v1 = pc.reshape(2, B, S, PC_W)
    oc = pl.pallas_call(
        _dil_kernel,
        grid=(B, 2),
        in_specs=[pl.BlockSpec((None, None, S, PC_W), lambda b, p: (p, b, 0, 0))],
        out_specs=pl.BlockSpec((None, S, LANES), lambda b, p: (b, 0, p)),
        out_shape=jax.ShapeDtypeStruct((B, S, GROUP_W), BF16),
        scratch_shapes=[
            pltpu.VMEM((4, S // 4, PC_W), BF16),
            pltpu.VMEM((16, S // 16, PC_W), BF16),
            pltpu.VMEM((4, S // 4, LANES), BF16),
            pltpu.VMEM((4, S // 4, LANES), F32),
            pltpu.VMEM((16, S // 16, LANES), BF16),
            pltpu.VMEM((16, S // 16, LANES), F32),
        ],
        compiler_params=_cparams(("parallel", "parallel"), VMEM_LIMIT),
        name="dilated_mixer",
    )(v1)
    return oc.reshape(B * S, GROUP_W)


FFN_HALO = 8
FFN_CHUNK = 256
FFN_ROWS = 256


def _ffn_kernel(x_ref, oa_ref, ob_ref, oc_ref, od_ref, g1_ref, sh_ref, sc_ref, g2_ref, lng_ref,
                wout_ref, wup_ref, cw_ref, cb_ref, wdn_ref, lnf_ref, o_ref, act_ref, zprev_ref,
                *, nps, final):
    tm = x_ref.shape[0]

    @pl.when((pl.program_id(0) % nps) == 0)
    def _():
        zprev_ref[...] = jnp.zeros_like(zprev_ref)

    cat = jnp.concatenate([oa_ref[...], ob_ref[...], oc_ref[...], od_ref[...]], axis=1)
    x1 = x_ref[...] + g1_ref[...] * _dot(cat, wout_ref[...])
    h = _rms_mod(x1, lng_ref[...], sc_ref[...], sh_ref[...]).astype(BF16)

    def conv(z, zp, cols):
        ze = jnp.concatenate([zp, z], axis=0)
        cw = cw_ref[:, cols]
        rb = z.shape[0]
        return (cw[2:3] * z + cw[1:2] * ze[FFN_HALO - 1:rb + FFN_HALO - 1]
                + cw[0:1] * ze[FFN_HALO - 2:rb + FFN_HALO - 2] + cb_ref[:, cols])

    for c in range(D_FF // FFN_CHUNK):
        ca = slice(c * FFN_CHUNK, (c + 1) * FFN_CHUNK)
        cb = slice(D_FF + c * FFN_CHUNK, D_FF + (c + 1) * FFN_CHUNK)
        pa = zprev_ref[:, ca]
        pb = zprev_ref[:, cb]
        for r in range(tm // FFN_ROWS):
            rows = slice(r * FFN_ROWS, (r + 1) * FFN_ROWS)
            za = _dot(h[rows], wup_ref[:, ca])
            zb = _dot(h[rows], wup_ref[:, cb])
            act_ref[rows, ca] = (_silu(conv(za, pa, ca)) * conv(zb, pb, cb)).astype(BF16)
            pa = za[FFN_ROWS - FFN_HALO:]
            pb = zb[FFN_ROWS - FFN_HALO:]
        zprev_ref[:, ca] = pa
        zprev_ref[:, cb] = pb
    y = x1 + g2_ref[...] * _dot(act_ref[...], wdn_ref[...])
    if final:
        ms = jnp.mean(y * y, axis=-1, keepdims=True)
        y = y * lax.rsqrt(ms + EPS) * lnf_ref[...]
    o_ref[...] = y


def _ffn(x2, mod4, l, oa, ob, oc, od, w_out, ln_g, w_up, conv_w, conv_b, w_down, lnf_g, S, tm, final):
    T, D = x2.shape
    nps = S // tm
    row = lambda w: pl.BlockSpec((tm, w), lambda i: (i, 0))
    modspec = lambda c: pl.BlockSpec((None, None, 1, D), lambda i: (l, i // nps, 0, c))
    single = functools.partial(_layer_spec, l, single=True)
    return pl.pallas_call(
        functools.partial(_ffn_kernel, nps=nps, final=final),
        grid=(T // tm,),
        in_specs=[
            row(D), row(GROUP_W), row(GROUP_W), row(GROUP_W), row(GROUP_W),
            modspec(2), modspec(3), modspec(4), modspec(5),
            single(1, D),
            single(D, D),
            single(D, 2 * D_FF),
            single(FFN_CONV_K, 2 * D_FF),
            single(1, 2 * D_FF),
            single(D_FF, D),
            pl.BlockSpec((1, D), lambda i: (0, 0), pipeline_mode=pl.Buffered(1)),
        ],
        out_specs=row(D),
        out_shape=jax.ShapeDtypeStruct((T, D), F32),
        scratch_shapes=[pltpu.VMEM((tm, D_FF), BF16), pltpu.VMEM((FFN_HALO, 2 * D_FF), F32)],
        compiler_params=_cparams(("arbitrary",), VMEM_LIMIT),
        name="outproj_ffn",
    )(x2, oa, ob, oc, od, mod4, mod4, mod4, mod4, ln_g, w_out, w_up, conv_w, conv_b, w_down, lnf_g)


def _rearrange_w_in(w_in):
    cut = SEC_B[0] + PB_W - LANES + GLA_RANK
    pad = jnp.zeros(w_in.shape[:-1] + (LANES - GLA_RANK,), w_in.dtype)
    return jnp.concatenate([w_in[..., :cut], pad, w_in[..., cut:]], axis=-1).astype(BF16)


def _rope_tables(S):
    inv = 1.0 / (ROPE_THETA ** (jnp.arange(0, HEAD_DIM, 2, dtype=F32) / HEAD_DIM))
    ang = jnp.arange(S, dtype=F32)[:, None] * inv[None, :]
    cos, sin = jnp.cos(ang), jnp.sin(ang)
    cos_t = jnp.concatenate([cos, cos, cos, cos], axis=-1)
    sin_t = jnp.concatenate([-sin, sin, -sin, sin], axis=-1)
    return cos_t, sin_t


def kernel(x, c, ada_w, ada_b, ln1_g, w_in, conv_w, conv_b, cln_g, cln_b, gla_w2, gla_b, gla_norm_g, sgu_ln_g, sgu_ln_b, sgu_w, sgu_b, w_out, ln2_g, ffn_up, ffn_conv_w, ffn_conv_b, ffn_down, lnf_g):
    B, S, D = x.shape
    L = ada_w.shape[0]
    T = B * S
    assert D == D_MODEL and S % (16 * DIL_BLOCK) == 0
    assert all(w // d == DIL_BLOCK for w, d in DIL_PAIRS)
    tm = 1024
    t_ffn = 1024
    t_gla = 256
    nbatch = 4 if B % 4 == 0 else 1

    mod4 = _modulation(c, ada_w, ada_b).reshape(L, B, 1, 6 * D)
    wp = _rearrange_w_in(w_in)
    w_out_b = w_out.astype(BF16)
    w_up_b = ffn_up.astype(BF16)
    w_dn_b = ffn_down.astype(BF16)
    w2p = jnp.pad(gla_w2, ((0, 0), (0, LANES - GLA_RANK), (0, 0))).astype(BF16)
    cos_t, sin_t = _rope_tables(S)
    sgu_bias = jnp.repeat(jnp.swapaxes(sgu_b, 1, 2), HEAD_DIM, axis=2)
    ng = jnp.tile(gla_norm_g, (1, GROUP_HEADS))
    vec = lambda a: a[:, None, :]

    x2 = x.reshape(T, D)
    for l in range(L):
        oa, pb, pc, od = _inproj(x2, mod4, l, vec(ln1_g), wp, cos_t, sin_t, sgu_w, sgu_bias,
                                 vec(sgu_ln_g), vec(sgu_ln_b), conv_w, vec(conv_b), vec(cln_g),
                                 vec(cln_b), S, tm)
        ob = _gla_mixer(pb, l, w2p, vec(gla_b), vec(ng), B, S, t_gla, nbatch)
        oc = _dilated_mixer(pc, B, S)
        x2 = _ffn(x2, mod4, l, oa, ob, oc, od, w_out_b, vec(ln2_g), w_up_b, ffn_conv_w,
                  vec(ffn_conv_b), w_dn_b, lnf_g[None, :], S, t_ffn, final=(l == L - 1))
    return x2.reshape(B, S, D)
```

```python
import functools

import numpy as np
import jax
import jax.numpy as jnp
from jax import lax
from jax.experimental import pallas as pl
from jax.experimental.pallas import tpu as pltpu

D_MODEL = 1024
GROUP_W = 256
GROUP_HEADS = 4
HEAD_DIM = 64
CONV_K = 31
GLA_DK = 32
GLA_RANK = 16
GLA_TAU = 16.0
GLA_CHUNK = 64
DIL_PAIRS = ((128, 1), (512, 4), (2048, 16))
DIL_BLOCK = 128
ROPE_THETA = 10000.0
SGU_CHUNK = 128
D_FF = 2816
FFN_CONV_K = 3
EPS = 1e-6
NEG_INF = -1e30
LOG2_E = 1.4426950408889634
Q_SCALE_LOG2 = HEAD_DIM ** -0.5 * LOG2_E

LANES = 128
BF16 = jnp.bfloat16
F32 = jnp.float32

W_AB = 1408
W_CD0 = 1296
SEC_A = (0, 512)
SEC_B = (512, 1408)
SEC_C = (0, 768)
SEC_D = (768, 1280)
PB_W = SEC_B[1] - SEC_B[0]
PC_W = 384
INPROJ_SUB = 256
CONV_HALO = 32
CONV_SUB = 64

VMEM_LIMIT = 56 * 1024 * 1024


def _cparams(sem, vmem=None):
    return pltpu.CompilerParams(dimension_semantics=sem, vmem_limit_bytes=vmem)


def _dot(a, b):
    return jnp.dot(a, b, preferred_element_type=F32)


def _dot_nt(a, b):
    return lax.dot_general(a, b, (((1,), (1,)), ((), ())), preferred_element_type=F32)


def _dot_tn(a, b):
    return lax.dot_general(a, b, (((0,), (0,)), ((), ())), preferred_element_type=F32)


def _sigmoid(x):
    return 1.0 / (1.0 + jnp.exp(-x))


def _silu(x):
    return x * _sigmoid(x)


def _gelu_tanh(x):
    c = np.sqrt(2.0 / np.pi).astype(np.float32)
    return x * (0.5 * (1.0 + jnp.tanh(c * (x + 0.044715 * (x * x * x)))))


def _rms_mod(x, g, scale, shift):
    ms = jnp.mean(x * x, axis=-1, keepdims=True)
    y = x * lax.rsqrt(ms + EPS) * g
    return y * (1.0 + scale) + shift


def _layernorm(x, g, b):
    mu = jnp.mean(x, axis=-1, keepdims=True)
    xc = x - mu
    var = jnp.mean(xc * xc, axis=-1, keepdims=True)
    return xc * lax.rsqrt(var + EPS) * g + b


def _mod_kernel(c_ref, w_ref, b_ref, o_ref):
    cond = _silu(c_ref[...]).astype(BF16)
    o_ref[...] = _dot(cond, w_ref[...].astype(BF16)) + b_ref[...]


def _modulation(c, ada_w, ada_b):
    L, D, D6 = ada_w.shape
    B = c.shape[0]
    n = D6 // D
    return pl.pallas_call(
        _mod_kernel,
        grid=(L, n),
        in_specs=[
            pl.BlockSpec((B, D), lambda l, j: (0, 0)),
            pl.BlockSpec((None, D, D), lambda l, j: (l, 0, j)),
            pl.BlockSpec((None, 1, D), lambda l, j: (l, 0, j)),
        ],
        out_specs=pl.BlockSpec((None, B, D), lambda l, j: (l, 0, j)),
        out_shape=jax.ShapeDtypeStruct((L, B, D6), F32),
        compiler_params=_cparams(("parallel", "parallel")),
        name="adaln_mod",
    )(c, ada_w, ada_b.reshape(L, 1, D6))


def _inproj_kernel(x_ref, sh_ref, sc_ref, g_ref, wab_ref, wcd_ref, cos_ref, sin_ref,
                   sw_ref, sbias_ref, slg_ref, slb_ref, cw_ref, cb_ref, clg_ref, clb_ref,
                   oa_ref, pb_ref, pc_ref, od_ref, ext_ref, *, nps):
    tm = x_ref.shape[0]
    ts = INPROJ_SUB

    @pl.when((pl.program_id(0) % nps) == 0)
    def _():
        ext_ref[0, 0:CONV_HALO, :] = jnp.zeros((CONV_HALO, GROUP_W), F32)

    lane = lax.broadcasted_iota(jnp.int32, (ts, LANES), 1)
    first_half = (lane % HEAD_DIM) < (HEAD_DIM // 2)
    row = lax.broadcasted_iota(jnp.int32, (SGU_CHUNK, SGU_CHUNK), 0)
    col = lax.broadcasted_iota(jnp.int32, (SGU_CHUNK, SGU_CHUNK), 1)
    causal = row >= col
    wm = [jnp.where(causal, sw_ref[g], 0.0).astype(BF16) for g in range(GROUP_HEADS)]
    grp = lax.broadcasted_iota(jnp.int32, (SGU_CHUNK, GROUP_W), 1) // HEAD_DIM
    sbias = sbias_ref[...]

    hs = []
    for sub in range(tm // ts):
        rs = slice(sub * ts, (sub + 1) * ts)
        h = _rms_mod(x_ref[rs, :], g_ref[...], sc_ref[...], sh_ref[...]).astype(BF16)

        za = _dot(h, wab_ref[:, SEC_A[0]:SEC_A[1]])
        ext_ref[0, CONV_HALO:, :] = za[:, :GROUP_W] * _sigmoid(za[:, GROUP_W:])
        n_shift = ts + CONV_HALO - 8
        for r in range(1, 8):
            ext_ref[r, 0:n_shift, :] = ext_ref[0, r:r + n_shift, :]
        off0 = CONV_HALO - (CONV_K - 1)
        for cs in range(ts // CONV_SUB):
            r0 = cs * CONV_SUB
            acc = jnp.zeros((CONV_SUB, GROUP_W), F32) + cb_ref[...]
            for k in range(CONV_K):
                o = off0 + k
                a0 = r0 + 8 * (o // 8)
                acc = acc + cw_ref[k:k + 1, :] * ext_ref[o % 8, a0:a0 + CONV_SUB, :]
            y = _layernorm(acc, clg_ref[...], clb_ref[...])
            oa_ref[sub * ts + r0:sub * ts + r0 + CONV_SUB, :] = _silu(y).astype(BF16)
        ext_ref[0, 0:CONV_HALO, :] = ext_ref[0, ts:ts + CONV_HALO, :]

        hs.append(h)

        zc = _dot(h, wcd_ref[:, SEC_C[0]:SEC_C[1]])
        cos_t = cos_ref[rs, :]
        sin_t = sin_ref[rs, :]

        def rope(t):
            partner = jnp.where(first_half, pltpu.roll(t, LANES - HEAD_DIM // 2, axis=1),
                                pltpu.roll(t, HEAD_DIM // 2, axis=1))
            return t * cos_t + partner * sin_t

        for p in range(2):
            q = rope(zc[:, p * LANES:(p + 1) * LANES]) * Q_SCALE_LOG2
            k = rope(zc[:, GROUP_W + p * LANES:GROUP_W + (p + 1) * LANES])
            v = zc[:, 2 * GROUP_W + p * LANES:2 * GROUP_W + (p + 1) * LANES]
            pc_ref[p, rs, 0:LANES] = q.astype(BF16)
            pc_ref[p, rs, LANES:2 * LANES] = k.astype(BF16)
            pc_ref[p, rs, 2 * LANES:3 * LANES] = v.astype(BF16)

        zd = _dot(h, wcd_ref[:, SEC_D[0]:SEC_D[1]])
        u = _gelu_tanh(zd[:, :GROUP_W])
        vn = _layernorm(_gelu_tanh(zd[:, GROUP_W:]), slg_ref[...], slb_ref[...]).astype(BF16)
        for c in range(ts // SGU_CHUNK):
            rows = slice(c * SGU_CHUNK, (c + 1) * SGU_CHUNK)
            vch = vn[rows]
            sv = sbias
            for g in range(GROUP_HEADS):
                sv = sv + jnp.where(grp == g, _dot(wm[g], vch), 0.0)
            od_ref[sub * ts + c * SGU_CHUNK:sub * ts + (c + 1) * SGU_CHUNK, :] = (u[rows] * sv).astype(BF16)

    for sub in range(tm // ts):
        pb_ref[sub * ts:(sub + 1) * ts, :] = _dot(hs[sub], wab_ref[:, SEC_B[0]:SEC_B[1]]).astype(BF16)


def _layer_spec(l, *tail, single=False):
    kw = dict(pipeline_mode=pl.Buffered(1)) if single else {}
    return pl.BlockSpec((None,) + tail, lambda *_: (l,) + (0,) * len(tail), **kw)


def _inproj(x2, mod4, l, ln_g, wab, wcd, cos_t, sin_t, sgu_w, sgu_bias, sgu_ln_g, sgu_ln_b,
            conv_w, conv_b, cln_g, cln_b, S, tm):
    T, D = x2.shape
    nps = S // tm
    return pl.pallas_call(
        functools.partial(_inproj_kernel, nps=nps),
        grid=(T // tm,),
        in_specs=[
            pl.BlockSpec((tm, D), lambda i: (i, 0)),
            pl.BlockSpec((None, None, 1, D), lambda i: (l, i // nps, 0, 0)),
            pl.BlockSpec((None, None, 1, D), lambda i: (l, i // nps, 0, 1)),
            _layer_spec(l, 1, D),
            _layer_spec(l, D, W_AB, single=True),
            _layer_spec(l, D, SEC_D[1], single=True),
            pl.BlockSpec((tm, LANES), lambda i: (i % nps, 0)),
            pl.BlockSpec((tm, LANES), lambda i: (i % nps, 0)),
            _layer_spec(l, GROUP_HEADS, SGU_CHUNK, SGU_CHUNK),
            _layer_spec(l, SGU_CHUNK, GROUP_W),
            _layer_spec(l, 1, GROUP_W),
            _layer_spec(l, 1, GROUP_W),
            _layer_spec(l, CONV_K, GROUP_W),
            _layer_spec(l, 1, GROUP_W),
            _layer_spec(l, 1, GROUP_W),
            _layer_spec(l, 1, GROUP_W),
        ],
        out_specs=[
            pl.BlockSpec((tm, GROUP_W), lambda i: (i, 0)),
            pl.BlockSpec((tm, PB_W), lambda i: (i, 0)),
            pl.BlockSpec((2, tm, PC_W), lambda i: (0, i, 0)),
            pl.BlockSpec((tm, GROUP_W), lambda i: (i, 0)),
        ],
        out_shape=[
            jax.ShapeDtypeStruct((T, GROUP_W), BF16),
            jax.ShapeDtypeStruct((T, PB_W), BF16),
            jax.ShapeDtypeStruct((2, T, PC_W), BF16),
            jax.ShapeDtypeStruct((T, GROUP_W), BF16),
        ],
        scratch_shapes=[pltpu.VMEM((8, INPROJ_SUB + CONV_HALO, GROUP_W), F32)],
        compiler_params=_cparams(("arbitrary",), VMEM_LIMIT),
        name="inproj",
    )(x2, mod4, mod4, ln_g, wab, wcd, cos_t, sin_t, sgu_w, sgu_bias, sgu_ln_g, sgu_ln_b,
      conv_w, conv_b, cln_g, cln_b)


def _gla_kernel(pb_ref, w2_ref, gb_ref, ng_ref, o_ref, st_ref):
    nbatch, tb = pb_ref.shape[0], pb_ref.shape[1]
    C = GLA_CHUNK
    HK = GROUP_HEADS * GLA_DK

    @pl.when(pl.program_id(1) == 0)
    def _():
        st_ref[...] = jnp.zeros_like(st_ref)

    r_i = lax.broadcasted_iota(jnp.int32, (tb, tb), 0)
    c_i = lax.broadcasted_iota(jnp.int32, (tb, tb), 1)
    ltri = jnp.where((r_i >= c_i) & (r_i // C == c_i // C), 1.0, 0.0).astype(BF16)
    att_r = lax.broadcasted_iota(jnp.int32, (C, GROUP_HEADS * C), 0)
    att_c = lax.broadcasted_iota(jnp.int32, (C, GROUP_HEADS * C), 1) % C
    att_causal = att_r >= att_c
    ks_mask = (lax.broadcasted_iota(jnp.int32, (GROUP_HEADS * C, HK), 0) // C
               == lax.broadcasted_iota(jnp.int32, (GROUP_HEADS * C, HK), 1) // GLA_DK)
    vs_mask = (lax.broadcasted_iota(jnp.int32, (GROUP_HEADS * C, GROUP_W), 0) // C
               == lax.broadcasted_iota(jnp.int32, (GROUP_HEADS * C, GROUP_W), 1) // HEAD_DIM)
    st_mask = (lax.broadcasted_iota(jnp.int32, (HK, GROUP_W), 0) // GLA_DK
               == lax.broadcasted_iota(jnp.int32, (HK, GROUP_W), 1) // HEAD_DIM)
    low_half = lax.broadcasted_iota(jnp.int32, (C, LANES), 1) < HEAD_DIM
    zero_bf = jnp.zeros((), BF16)

    w2 = w2_ref[...]
    gb = gb_ref[...]
    ng = ng_ref[...]

    for bb in range(nbatch):
        q = pb_ref[bb, :, 0:HK].astype(F32)
        k = pb_ref[bb, :, HK:2 * HK].astype(F32)
        lr = pb_ref[bb, :, 2 * HK + 2 * GROUP_W:2 * HK + 2 * GROUP_W + LANES]
        pre = _dot(lr, w2) + gb
        gk = (jnp.minimum(pre, 0.0) - jnp.log(1.0 + jnp.exp(-jnp.abs(pre)))) * (1.0 / GLA_TAU)
        g1 = gk.astype(BF16)
        rem = gk - g1.astype(F32)
        g2 = rem.astype(BF16)
        g3 = (rem - g2.astype(F32)).astype(BF16)
        b = _dot(ltri, g1) + _dot(ltri, g2) + _dot(ltri, g3)
        q_t = (q * (GLA_DK ** -0.5) * jnp.exp(b)).astype(BF16)
        k_t = (k * jnp.exp(-b)).astype(BF16)
        st = st_ref[bb]
        for ch in range(tb // C):
            rows = slice(ch * C, (ch + 1) * C)
            v = pb_ref[bb, rows, 2 * HK:2 * HK + GROUP_W]
            g = pb_ref[bb, rows, 2 * HK + GROUP_W:2 * HK + 2 * GROUP_W].astype(F32)
            b_c = b[rows]
            b_last = b_c[C - 1:C, :]
            k_end = (k[rows] * jnp.exp(b_last - b_c)).astype(BF16)
            q_c = q_t[rows]
            k_stack = jnp.where(ks_mask, jnp.concatenate([k_t[rows]] * GROUP_HEADS, axis=0), zero_bf)
            att = jnp.where(att_causal, _dot_nt(q_c, k_stack), 0.0).astype(BF16)
            v_stack = jnp.where(vs_mask, jnp.concatenate([v] * GROUP_HEADS, axis=0), zero_bf)
            o = _dot(att, v_stack) + _dot(q_c, st.astype(BF16))
            dec_col = jnp.exp(jnp.transpose(b_c[C - 8:C, :])[:, 7:8])
            st = st * dec_col + jnp.where(st_mask, _dot_tn(k_end, v), 0.0)
            invs = []
            for grp in range(GROUP_W // LANES):
                og = o[:, grp * LANES:(grp + 1) * LANES]
                o2 = og * og
                lo = jnp.sum(jnp.where(low_half, o2, 0.0), axis=-1, keepdims=True) * (1.0 / HEAD_DIM)
                hi = jnp.sum(jnp.where(low_half, 0.0, o2), axis=-1, keepdims=True) * (1.0 / HEAD_DIM)
                invs.append(jnp.where(low_half, lax.rsqrt(lo + EPS), lax.rsqrt(hi + EPS)))
            inv = jnp.concatenate(invs, axis=1)
            o_ref[bb, rows, :] = (o * inv * ng * _silu(g)).astype(BF16)
        st_ref[bb] = st


def _gla_mixer(pb, l, w2p, gb, ng, B, S, tb, nbatch):
    T = pb.shape[0]
    nps = S // tb
    out = pl.pallas_call(
        _gla_kernel,
        grid=(B // nbatch, nps),
        in_specs=[
            pl.BlockSpec((nbatch, tb, PB_W), lambda b, i: (b, i, 0)),
            _layer_spec(l, LANES, LANES),
            _layer_spec(l, 1, LANES),
            _layer_spec(l, 1, GROUP_W),
        ],
        out_specs=pl.BlockSpec((nbatch, tb, GROUP_W), lambda b, i: (b, i, 0)),
        out_shape=jax.ShapeDtypeStruct((B, S, GROUP_W), BF16),
        scratch_shapes=[pltpu.VMEM((nbatch, GROUP_HEADS * GLA_DK, GROUP_W), F32)],
        compiler_params=_cparams(("arbitrary", "arbitrary")),
        name="gla_mixer",
    )(pb.reshape(B, S, PB_W), w2p, gb, ng)
    return out.reshape(T, GROUP_W)


PERM = 256
DIL_UNROLL = 16
COMB_UNROLL = 4


def _perm_matrix(dil, transpose=False):
    i = lax.broadcasted_iota(jnp.int32, (PERM, PERM), 0)
    j = lax.broadcasted_iota(jnp.int32, (PERM, PERM), 1)
    if transpose:
        i, j = j, i
    g = PERM // dil
    return jnp.where(j == (i % g) * dil + i // g, 1.0, 0.0).astype(BF16)


def _dil_kernel(pc_ref, o_ref, p4_ref, p16_ref, ro4_ref, rl4_ref, ro16_ref, rl16_ref):
    S = pc_ref.shape[0]
    BLK = DIL_BLOCK
    nblk = S // BLK
    qi = lax.broadcasted_iota(jnp.int32, (2 * BLK, 2 * BLK), 0) % BLK
    kj = lax.broadcasted_iota(jnp.int32, (2 * BLK, 2 * BLK), 1)
    band = (kj >= qi) & (kj <= qi + BLK)
    lane = lax.broadcasted_iota(jnp.int32, (BLK, LANES), 1)
    head0 = lane < HEAD_DIM
    zero_bf = jnp.zeros((), BF16)
    strided = ((4, p4_ref, ro4_ref, rl4_ref), (16, p16_ref, ro16_ref, rl16_ref))

    def attend(src, blk):
        row0 = pl.multiple_of(blk * BLK, BLK)
        prow0 = pl.multiple_of(jnp.maximum(blk - 1, 0) * BLK, BLK)
        q = src[pl.ds(row0, BLK), 0:LANES]
        qs = jnp.concatenate([jnp.where(head0, q, zero_bf), jnp.where(head0, zero_bf, q)], axis=0)
        kk = jnp.concatenate([src[pl.ds(prow0, BLK), LANES:2 * LANES],
                              src[pl.ds(row0, BLK), LANES:2 * LANES]], axis=0)
        vv = jnp.concatenate([src[pl.ds(prow0, BLK), 2 * LANES:3 * LANES],
                              src[pl.ds(row0, BLK), 2 * LANES:3 * LANES]], axis=0)
        mask = band & (kj >= jnp.where(blk == 0, BLK, 0))
        s = jnp.where(mask, _dot_nt(qs, kk), NEG_INF)
        m = jnp.max(s, axis=-1, keepdims=True)
        p = jnp.exp2(s - m)
        den = jnp.sum(p, axis=-1, keepdims=True)
        on = _dot(p.astype(BF16), vv) / den
        lse = m + jnp.log2(den)
        return (jnp.where(head0, on[:BLK], on[BLK:]).astype(BF16),
                jnp.where(head0, lse[:BLK], lse[BLK:]))

    perms = [_perm_matrix(d) for d, _, _, _ in strided]

    def perm_body(it, carry):
        for u in range(2):
            blk = it * 2 + u
            xblk = pc_ref[pl.ds(pl.multiple_of(blk * PERM, PERM), PERM), :]
            for (dil, dst, _, _), pm in zip(strided, perms):
                g = PERM // dil
                y = _dot(pm, xblk).astype(BF16)
                g0 = pl.multiple_of(blk * g, g)
                for r in range(dil):
                    dst[r, pl.ds(g0, g), :] = y[r * g:(r + 1) * g]
        return carry

    lax.fori_loop(0, S // PERM // 2, perm_body, 0)

    for dil, src, ro, rl in strided:
        nb = nblk // dil

        def body(it, carry, nb=nb, src=src, ro=ro, rl=rl):
            for u in range(DIL_UNROLL):
                flat = it * DIL_UNROLL + u
                r = flat // nb
                blk = flat % nb
                o, l = attend(src.at[r], blk)
                row0 = pl.multiple_of(blk * BLK, BLK)
                ro[r, pl.ds(row0, BLK), :] = o
                rl[r, pl.ds(row0, BLK), :] = l
            return carry

        lax.fori_loop(0, nblk // DIL_UNROLL, body, 0)

    unperms = [_perm_matrix(d, transpose=True) for d, _, _, _ in strided]

    def comb_body(it, carry):
        for u in range(COMB_UNROLL):
            blk = it * COMB_UNROLL + u
            outs, lses = [], []
            for h in range(PERM // BLK):
                o, l = attend(pc_ref, blk * (PERM // BLK) + h)
                outs.append(o)
                lses.append(l)
            branches = [(jnp.concatenate(outs, axis=0).astype(F32), jnp.concatenate(lses, axis=0))]
            for (dil, _, ro, rl), um in zip(strided, unperms):
                g = PERM // dil
                g0 = pl.multiple_of(blk * g, g)
                go = jnp.concatenate([ro[r, pl.ds(g0, g), :] for r in range(dil)], axis=0)
                gl = jnp.concatenate([rl[r, pl.ds(g0, g), :] for r in range(dil)], axis=0)
                hi = gl.astype(BF16)
                rem = gl - hi.astype(F32)
                mid = rem.astype(BF16)
                lo = (rem - mid.astype(F32)).astype(BF16)
                a = _dot(um, jnp.concatenate([go, hi], axis=1))
                b = _dot(um, jnp.concatenate([mid, lo], axis=1))
                branches.append((a[:, :LANES], (a[:, LANES:] + b[:, :LANES]) + b[:, LANES:]))
            m = jnp.maximum(jnp.maximum(branches[0][1], branches[1][1]), branches[2][1])
            es = [jnp.exp2(l - m) for _, l in branches]
            den = es[0] + es[1] + es[2]
            num = es[0] * branches[0][0] + es[1] * branches[1][0] + es[2] * branches[2][0]
            o_ref[pl.ds(pl.multiple_of(blk * PERM, PERM), PERM), :] = (num / den).astype(BF16)
        return carry

    lax.fori_loop(0, S // PERM // COMB_UNROLL, comb_body, 0)


def _dilated_mixer(pc, B, S):
    v1 = pc.reshape(2, B, S, PC_W)
    oc = pl.pallas_call(
        _dil_kernel,
        grid=(B, 2),
        in_specs=[pl.BlockSpec((None, None, S, PC_W), lambda b, p: (p, b, 0, 0))],
        out_specs=pl.BlockSpec((None, S, LANES), lambda b, p: (b, 0, p)),
        out_shape=jax.ShapeDtypeStruct((B, S, GROUP_W), BF16),
        scratch_shapes=[
            pltpu.VMEM((4, S // 4, PC_W), BF16),
            pltpu.VMEM((16, S // 16, PC_W), BF16),
            pltpu.VMEM((4, S // 4, LANES), BF16),
            pltpu.VMEM((4, S // 4, LANES), F32),
            pltpu.VMEM((16, S // 16, LANES), BF16),
            pltpu.VMEM((16, S // 16, LANES), F32),
        ],
        compiler_params=_cparams(("parallel", "parallel"), VMEM_LIMIT),
        name="dilated_mixer",
    )(v1)
    return oc.reshape(B * S, GROUP_W)


FFN_HALO = 8
FFN_CHUNK = 256
FFN_ROWS = 256


def _ffn_kernel(x_ref, oa_ref, ob_ref, oc_ref, od_ref, g1_ref, sh_ref, sc_ref, g2_ref, lng_ref,
                wout_ref, wup_ref, cw_ref, cb_ref, wdn_ref, lnf_ref, o_ref, act_ref, zprev_ref,
                *, nps, final):
    tm = x_ref.shape[0]

    @pl.when((pl.program_id(0) % nps) == 0)
    def _():
        zprev_ref[...] = jnp.zeros_like(zprev_ref)

    cat = jnp.concatenate([oa_ref[...], ob_ref[...], oc_ref[...], od_ref[...]], axis=1)
    x1 = x_ref[...] + g1_ref[...] * _dot(cat, wout_ref[...])
    h = _rms_mod(x1, lng_ref[...], sc_ref[...], sh_ref[...]).astype(BF16)

    def conv(z, zp, cols):
        ze = jnp.concatenate([zp, z], axis=0)
        cw = cw_ref[:, cols]
        rb = z.shape[0]
        return (cw[2:3] * z + cw[1:2] * ze[FFN_HALO - 1:rb + FFN_HALO - 1]
                + cw[0:1] * ze[FFN_HALO - 2:rb + FFN_HALO - 2] + cb_ref[:, cols])

    for c in range(D_FF // FFN_CHUNK):
        ca = slice(c * FFN_CHUNK, (c + 1) * FFN_CHUNK)
        cb = slice(D_FF + c * FFN_CHUNK, D_FF + (c + 1) * FFN_CHUNK)
        pa = zprev_ref[:, ca]
        pb = zprev_ref[:, cb]
        for r in range(tm // FFN_ROWS):
            rows = slice(r * FFN_ROWS, (r + 1) * FFN_ROWS)
            za = _dot(h[rows], wup_ref[:, ca])
            zb = _dot(h[rows], wup_ref[:, cb])
            act_ref[rows, ca] = (_silu(conv(za, pa, ca)) * conv(zb, pb, cb)).astype(BF16)
            pa = za[FFN_ROWS - FFN_HALO:]
            pb = zb[FFN_ROWS - FFN_HALO:]
        zprev_ref[:, ca] = pa
        zprev_ref[:, cb] = pb
    y = x1 + g2_ref[...] * _dot(act_ref[...], wdn_ref[...])
    if final:
        ms = jnp.mean(y * y, axis=-1, keepdims=True)
        y = y * lax.rsqrt(ms + EPS) * lnf_ref[...]
    o_ref[...] = y


def _ffn(x2, mod4, l, oa, ob, oc, od, w_out, ln_g, w_up, conv_w, conv_b, w_down, lnf_g, S, tm, final):
    T, D = x2.shape
    nps = S // tm
    row = lambda w: pl.BlockSpec((tm, w), lambda i: (i, 0))
    modspec = lambda c: pl.BlockSpec((None, None, 1, D), lambda i: (l, i // nps, 0, c))
    single = functools.partial(_layer_spec, l, single=True)
    return pl.pallas_call(
        functools.partial(_ffn_kernel, nps=nps, final=final),
        grid=(T // tm,),
        in_specs=[
            row(D), row(GROUP_W), row(GROUP_W), row(GROUP_W), row(GROUP_W),
            modspec(2), modspec(3), modspec(4), modspec(5),
            single(1, D),
            single(D, D),
            single(D, 2 * D_FF),
            single(FFN_CONV_K, 2 * D_FF),
            single(1, 2 * D_FF),
            single(D_FF, D),
            pl.BlockSpec((1, D), lambda i: (0, 0), pipeline_mode=pl.Buffered(1)),
        ],
        out_specs=row(D),
        out_shape=jax.ShapeDtypeStruct((T, D), F32),
        scratch_shapes=[pltpu.VMEM((tm, D_FF), BF16), pltpu.VMEM((FFN_HALO, 2 * D_FF), F32)],
        compiler_params=_cparams(("arbitrary",), VMEM_LIMIT),
        name="outproj_ffn",
    )(x2, oa, ob, oc, od, mod4, mod4, mod4, mod4, ln_g, w_out, w_up, conv_w, conv_b, w_down, lnf_g)


def _rope_tables(S):
    inv = 1.0 / (ROPE_THETA ** (jnp.arange(0, HEAD_DIM, 2, dtype=F32) / HEAD_DIM))
    ang = jnp.arange(S, dtype=F32)[:, None] * inv[None, :]
    cos, sin = jnp.cos(ang), jnp.sin(ang)
    cos_t = jnp.concatenate([cos, cos, cos, cos], axis=-1)
    sin_t = jnp.concatenate([-sin, sin, -sin, sin], axis=-1)
    return cos_t, sin_t


def kernel(x, c, ada_w, ada_b, ln1_g, w_in, conv_w, conv_b, cln_g, cln_b, gla_w2, gla_b, gla_norm_g, sgu_ln_g, sgu_ln_b, sgu_w, sgu_b, w_out, ln2_g, ffn_up, ffn_conv_w, ffn_conv_b, ffn_down, lnf_g):
    B, S, D = x.shape
    L = ada_w.shape[0]
    T = B * S
    assert D == D_MODEL and S % (16 * DIL_BLOCK) == 0
    assert all(w // d == DIL_BLOCK for w, d in DIL_PAIRS)
    tm = 1024
    t_ffn = 1024
    t_gla = 256
    nbatch = 4 if B % 4 == 0 else 1

    mod4 = _modulation(c, ada_w, ada_b).reshape(L, B, 1, 6 * D)
    wab = w_in[..., :W_AB].astype(BF16)
    wcd = w_in[..., W_CD0:].astype(BF16)
    w_out_b = w_out.astype(BF16)
    w_up_b = ffn_up.astype(BF16)
    w_dn_b = ffn_down.astype(BF16)
    w2p = jnp.pad(gla_w2, ((0, 0), (0, LANES - GLA_RANK), (0, 0))).astype(BF16)
    cos_t, sin_t = _rope_tables(S)
    sgu_bias = jnp.repeat(jnp.swapaxes(sgu_b, 1, 2), HEAD_DIM, axis=2)
    ng = jnp.tile(gla_norm_g, (1, GROUP_HEADS))
    vec = lambda a: a[:, None, :]

    x2 = x.reshape(T, D)
    for l in range(L):
        oa, pb, pc, od = _inproj(x2, mod4, l, vec(ln1_g), wab, wcd, cos_t, sin_t, sgu_w, sgu_bias,
                                 vec(sgu_ln_g), vec(sgu_ln_b), conv_w, vec(conv_b), vec(cln_g),
                                 vec(cln_b), S, tm)
        ob = _gla_mixer(pb, l, w2p, vec(gla_b), vec(ng), B, S, t_gla, nbatch)
        oc = _dilated_mixer(pc, B, S)
        x2 = _ffn(x2, mod4, l, oa, ob, oc, od, w_out_b, vec(ln2_g), w_up_b, ffn_conv_w,
                  vec(ffn_conv_b), w_dn_b, lnf_g[None, :], S, t_ffn, final=(l == L - 1))
    return x2.reshape(B, S, D)
```
